```python
import jax, jax.numpy as jnp
from jax import lax
import numpy as np

D_MODEL = 2048
BATCH = 4
SEQ = 4096
DEPTH = 2

HEAD_DIM = 128
SB_HEADS = 6
SB_WIDTH = SB_HEADS * HEAD_DIM
SC_GROUPS = 4
SC_WIDTH = SC_GROUPS * HEAD_DIM
GDN_HEADS = 6
GDN_WIDTH = GDN_HEADS * HEAD_DIM
D_MIX = SB_WIDTH + SC_WIDTH + GDN_WIDTH
SC_CONV = 3
GDN_CONV = 4
GDN_CHUNK = 64
SB_BLOCK = 128
D_FF = 4 * D_MODEL
RMS_EPS = 1e-6
L2_EPS = 1e-6
IN_DIM = 3 * SB_WIDTH + 3 * SC_WIDTH + 3 * GDN_WIDTH + GDN_WIDTH + 2 * GDN_HEADS

kernel_name = "hybrid_sb_conv_gdn_block"


def rmsnorm(x, w):
    xf = x.astype(jnp.float32)
    r = lax.rsqrt(jnp.mean(xf * xf, axis=-1, keepdims=True) + RMS_EPS)
    return (xf * r * w.astype(jnp.float32)).astype(x.dtype)


def l2norm(x):
    xf = x.astype(jnp.float32)
    return xf * lax.rsqrt(jnp.sum(xf * xf, axis=-1, keepdims=True) + L2_EPS)


def causal_depthwise_conv(u, w):
    K = w.shape[0]
    T = u.shape[1]
    up = jnp.pad(u, ((0, 0), (K - 1, 0), (0, 0)))
    return sum(w[i] * up[:, i:i + T] for i in range(K))


def stick_breaking_attention(q, k, v):
    B_, T, H, Dh = q.shape
    nb = T // SB_BLOCK
    scale = Dh ** -0.5
    qb = q.reshape(B_, nb, SB_BLOCK, H, Dh).transpose(1, 0, 3, 2, 4)
    k_pos = jnp.arange(T)

    def one_block(args):
        q_blk, blk = args
        q_pos = blk * SB_BLOCK + jnp.arange(SB_BLOCK)
        z = jnp.einsum('bhqd,bkhd->bhqk', q_blk, k).astype(jnp.float32) * scale
        causal = k_pos[None, :] < q_pos[:, None]
        log_fail = jnp.where(causal, -jax.nn.softplus(z), 0.0)
        rem = lax.cumsum(log_fail, axis=3, reverse=True) - log_fail
        w = jnp.where(causal, jnp.exp(jax.nn.log_sigmoid(z) + rem), 0.0)
        return jnp.einsum('bhqk,bkhd->bqhd', w.astype(v.dtype), v)

    out = lax.map(one_block, (qb, jnp.arange(nb)))
    return out.transpose(1, 0, 2, 3, 4).reshape(B_, T, H * Dh)


def gated_delta_rule(q, k, v, g, beta):
    B_, T, H, Dk = q.shape
    Dv = v.shape[-1]
    C = GDN_CHUNK
    N = T // C
    f32 = jnp.float32

    def chunks(a):
        a = a.astype(f32).reshape((B_, N, C, H) + a.shape[3:])
        return jnp.moveaxis(a, 3, 1)

    q = chunks(q) * (Dk ** -0.5)
    k = chunks(k)
    v = chunks(v)
    g = jnp.cumsum(chunks(g), axis=-1)
    beta = chunks(beta)
    tri = jnp.tril(jnp.ones((C, C), bool))
    strict = jnp.tril(jnp.ones((C, C), bool), -1)
    diff = g[..., :, None] - g[..., None, :]
    decay = jnp.where(tri, jnp.exp(jnp.where(tri, diff, 0.0)), 0.0)
    k_beta = k * beta[..., None]
    v_beta = v * beta[..., None]
    m = jnp.where(strict, jnp.einsum('bhnid,bhnjd->bhnij', k_beta, k) * decay, 0.0)
    eye = jnp.eye(C, dtype=f32)
    t_mat = lax.linalg.triangular_solve(eye + m, jnp.broadcast_to(eye, m.shape),
                                        left_side=True, lower=True, unit_diagonal=True)
    u = t_mat @ v_beta
    w = t_mat @ (k_beta * jnp.exp(g)[..., None])
    attn_intra = jnp.where(tri, jnp.einsum('bhnid,bhnjd->bhnij', q, k) * decay, 0.0)
    g_last = g[..., -1]
    q_dec = q * jnp.exp(g)[..., None]
    k_dec = k * jnp.exp(g_last[..., None] - g)[..., None]

    def step(state, xs):
        q_c, k_c, u_c, w_c, a_c, gl_c = xs
        v_new = u_c - w_c @ state
        o = q_c @ state + a_c @ v_new
        state = state * jnp.exp(gl_c)[..., None, None] + jnp.swapaxes(k_c, -1, -2) @ v_new
        return state, o

    xs = tuple(jnp.moveaxis(a, 2, 0) for a in (q_dec, k_dec, u, w, attn_intra, g_last))
    state0 = jnp.zeros((B_, H, Dk, Dv), f32)
    _, o = lax.scan(step, state0, xs)
    return o.transpose(1, 0, 3, 2, 4).reshape(B_, T, H, Dv)


def gdn_mixer(qkv, z, a, b, conv_w, a_log, dt_bias, norm_w):
    B_, T, _ = qkv.shape
    qkv = jax.nn.silu(causal_depthwise_conv(qkv, conv_w))
    q, k, v = jnp.split(qkv, 3, axis=-1)
    q = l2norm(q.reshape(B_, T, GDN_HEADS, HEAD_DIM))
    k = l2norm(k.reshape(B_, T, GDN_HEADS, HEAD_DIM))
    v = v.reshape(B_, T, GDN_HEADS, HEAD_DIM)
    g = -jnp.exp(a_log.astype(jnp.float32)) * jax.nn.softplus(a.astype(jnp.float32) + dt_bias.astype(jnp.float32))
    beta = jax.nn.sigmoid(b.astype(jnp.float32))
    o = gated_delta_rule(q, k, v, g, beta)
    o = rmsnorm(o, norm_w) * jax.nn.silu(z.astype(jnp.float32).reshape(B_, T, GDN_HEADS, HEAD_DIM))
    return o.reshape(B_, T, GDN_WIDTH).astype(qkv.dtype)


def hybrid_layer(x, norm1_w, w_in, sc_conv_w, gdn_conv_w, gdn_a_log, gdn_dt_bias, gdn_norm_w,
                 w_out, norm2_w, w_up, w_down):
    B_, T, _ = x.shape
    h = rmsnorm(x, norm1_w)
    proj = h @ w_in
    sizes = [SB_WIDTH, SB_WIDTH, SB_WIDTH, SC_WIDTH, SC_WIDTH, SC_WIDTH,
             3 * GDN_WIDTH, GDN_WIDTH, GDN_HEADS, GDN_HEADS]
    cuts = [int(c) for c in np.cumsum(sizes)[:-1]]
    sb_q, sb_k, sb_v, sc_b, sc_c, sc_h, gdn_qkv, gdn_z, gdn_a, gdn_b = jnp.split(proj, cuts, axis=-1)
    heads = lambda t: t.reshape(B_, T, SB_HEADS, HEAD_DIM)
    y_sb = stick_breaking_attention(heads(sb_q), heads(sb_k), heads(sb_v))
    y_sc = sc_b * causal_depthwise_conv(sc_c * sc_h, sc_conv_w)
    y_gdn = gdn_mixer(gdn_qkv, gdn_z, gdn_a, gdn_b, gdn_conv_w, gdn_a_log, gdn_dt_bias, gdn_norm_w)
    y = jnp.concatenate([y_sb.astype(x.dtype), y_sc, y_gdn], axis=-1)
    x = x + y @ w_out
    h2 = rmsnorm(x, norm2_w)
    return x + jnp.square(jax.nn.relu(h2 @ w_up)) @ w_down


def setup_inputs(seed: int = 0) -> dict:
    key = jax.random.key(seed)
    ks = jax.random.split(key, 14)
    f32 = jnp.float32
    x = jax.random.normal(ks[0], (BATCH, SEQ, D_MODEL), f32)
    norm1_w = 1.0 + 0.02 * jax.random.normal(ks[1], (DEPTH, D_MODEL), f32)
    w_in = jax.random.normal(ks[2], (DEPTH, D_MODEL, IN_DIM), f32) * D_MODEL ** -0.5
    sc_conv_w = jax.random.normal(ks[3], (DEPTH, SC_CONV, SC_WIDTH), f32) * SC_CONV ** -0.5
    gdn_conv_w = jax.random.normal(ks[4], (DEPTH, GDN_CONV, 3 * GDN_WIDTH), f32) * GDN_CONV ** -0.5
    gdn_a_log = jnp.log(jax.random.uniform(ks[5], (DEPTH, GDN_HEADS), f32, 1.0, 16.0))
    dt = jnp.exp(jax.random.uniform(ks[6], (DEPTH, GDN_HEADS), f32, np.log(1e-3), np.log(1e-1)))
    gdn_dt_bias = dt + jnp.log(-jnp.expm1(-dt))
    gdn_norm_w = 1.0 + 0.02 * jax.random.normal(ks[7], (DEPTH, HEAD_DIM), f32)
    w_out = jax.random.normal(ks[8], (DEPTH, D_MIX, D_MODEL), f32) * D_MIX ** -0.5
    norm2_w = 1.0 + 0.02 * jax.random.normal(ks[9], (DEPTH, D_MODEL), f32)
    w_up = jax.random.normal(ks[10], (DEPTH, D_MODEL, D_FF), f32) * D_MODEL ** -0.5
    w_down = jax.random.normal(ks[11], (DEPTH, D_FF, D_MODEL), f32) * D_FF ** -0.5
    final_norm_w = 1.0 + 0.02 * jax.random.normal(ks[12], (D_MODEL,), f32)
    return {"x": x, "norm1_w": norm1_w, "w_in": w_in, "sc_conv_w": sc_conv_w,
            "gdn_conv_w": gdn_conv_w, "gdn_a_log": gdn_a_log, "gdn_dt_bias": gdn_dt_bias,
            "gdn_norm_w": gdn_norm_w, "w_out": w_out, "norm2_w": norm2_w, "w_up": w_up,
            "w_down": w_down, "final_norm_w": final_norm_w}


def reference(x, norm1_w, w_in, sc_conv_w, gdn_conv_w, gdn_a_log, gdn_dt_bias, gdn_norm_w,
              w_out, norm2_w, w_up, w_down, final_norm_w):
    for l in range(DEPTH):
        x = hybrid_layer(x, norm1_w[l], w_in[l], sc_conv_w[l], gdn_conv_w[l], gdn_a_log[l],
                         gdn_dt_bias[l], gdn_norm_w[l], w_out[l], norm2_w[l], w_up[l], w_down[l])
    return rmsnorm(x, final_norm_w)
```

```python
import functools

import jax
import jax.numpy as jnp
from jax import lax
from jax.experimental import pallas as pl
from jax.experimental.pallas import tpu as pltpu

HEAD_DIM = 128
SB_HEADS = 6
SB_WIDTH = SB_HEADS * HEAD_DIM
SC_WIDTH = 4 * HEAD_DIM
GDN_HEADS = 6
GDN_WIDTH = GDN_HEADS * HEAD_DIM
SC_CONV = 3
GDN_CONV = 4
GDN_CHUNK = 64
RMS_EPS = 1e-6
L2_EPS = 1e-6
HALO = 8
VMEM_LIMIT_BYTES = 56 * 1024 * 1024

F32 = jnp.float32
BF16 = jnp.bfloat16


def _cparams(*sem):
    return pltpu.CompilerParams(dimension_semantics=sem, vmem_limit_bytes=VMEM_LIMIT_BYTES)


def _dot(a, b):
    return jnp.dot(a, b, preferred_element_type=F32)


def _dot_nt(a, b):
    return lax.dot_general(a, b, (((1,), (1,)), ((), ())), preferred_element_type=F32)


def _dot_tn(a, b):
    return lax.dot_general(a, b, (((0,), (0,)), ((), ())), preferred_element_type=F32)


def _norm_matmul_kernel(x_ref, nw_ref, w_ref, o_ref, hn_ref, *, sq_relu):
    @pl.when(pl.program_id(1) == 0)
    def _():
        x = x_ref[...]
        r = lax.rsqrt(jnp.mean(x * x, axis=-1, keepdims=True) + RMS_EPS)
        hn_ref[...] = (x * r * nw_ref[...]).astype(BF16)

    acc = _dot(hn_ref[...], w_ref[...])
    if sq_relu:
        acc = jnp.square(jnp.maximum(acc, 0.0))
    o_ref[...] = acc.astype(o_ref.dtype)


def _norm_matmul(x, norm_w, w, out_dtype, *, tm, tn, sq_relu=False):
    m, d = x.shape
    n = w.shape[1]
    tm, tn = min(tm, m), min(tn, n)
    return pl.pallas_call(
        functools.partial(_norm_matmul_kernel, sq_relu=sq_relu),
        grid=(m // tm, n // tn),
        in_specs=[pl.BlockSpec((tm, d), lambda i, j: (i, 0)),
                  pl.BlockSpec((1, d), lambda i, j: (0, 0)),
                  pl.BlockSpec((d, tn), lambda i, j: (0, j))],
        out_specs=pl.BlockSpec((tm, tn), lambda i, j: (i, j)),
        out_shape=jax.ShapeDtypeStruct((m, n), out_dtype),
        scratch_shapes=[pltpu.VMEM((tm, d), BF16)],
        compiler_params=_cparams("parallel", "arbitrary"),
        name="norm_matmul",
    )(x, norm_w.reshape(1, d), w)


def _softplus(z):
    return jnp.maximum(z, 0.0) + jnp.log1p(jnp.exp(-jnp.abs(z)))


def _attn_kernel(q_ref, k_ref, v_ref, o_ref, *, blk, scale):
    i = pl.program_id(2)
    q = q_ref[...]
    row = lax.broadcasted_iota(jnp.int32, (blk, blk), 0)
    col = lax.broadcasted_iota(jnp.int32, (blk, blk), 1)
    suffix_ones = jnp.where(row >= col, 1.0, 0.0).astype(BF16)
    causal = col < row

    def block(j, carry, acc, masked):
        start = pl.multiple_of(j * blk, blk)
        kb = k_ref[pl.ds(start, blk), :]
        vb = v_ref[pl.ds(start, blk), :]
        z = _dot_nt(q, kb) * scale
        sp = _softplus(z)
        if masked:
            sp = jnp.where(causal, sp, 0.0)
        hi = sp.astype(BF16)
        lo = (sp - hi.astype(F32)).astype(BF16)
        rem = _dot(hi, suffix_ones) + _dot(lo, suffix_ones) + carry
        w = jnp.exp(z - rem)
        if masked:
            w = jnp.where(causal, w, 0.0)
        acc = acc + _dot(w.astype(BF16), vb)
        return rem[:, 0:1], acc

    carry0 = jnp.zeros((blk, 1), F32)
    acc0 = jnp.zeros((blk, HEAD_DIM), F32)
    carry, acc = block(i, carry0, acc0, True)

    def body(jj, c):
        return block(i - 1 - jj, c[0], c[1], False)

    carry, acc = lax.fori_loop(0, i, body, (carry, acc))
    o_ref[...] = acc.astype(o_ref.dtype)


def _sb_attention(qkv, batch, seq, *, blk):
    blk = min(blk, seq)
    nq = seq // blk
    h = SB_HEADS
    return pl.pallas_call(
        functools.partial(_attn_kernel, blk=blk, scale=HEAD_DIM ** -0.5),
        grid=(batch, h, nq),
        in_specs=[pl.BlockSpec((blk, HEAD_DIM), lambda b, hh, i: (b * nq + i, hh)),
                  pl.BlockSpec((seq, HEAD_DIM), lambda b, hh, i: (b, h + hh)),
                  pl.BlockSpec((seq, HEAD_DIM), lambda b, hh, i: (b, 2 * h + hh))],
        out_specs=pl.BlockSpec((blk, HEAD_DIM), lambda b, hh, i: (b * nq + i, hh)),
        out_shape=jax.ShapeDtypeStruct((batch * seq, SB_WIDTH), BF16),
        compiler_params=_cparams("parallel", "parallel", "arbitrary"),
        name="sb_attention",
    )(qkv, qkv, qkv)


def _halo_index(rows_per_tile):
    per = rows_per_tile // HALO
    return lambda t: jnp.maximum(t * per - 1, 0)


def _sconv_kernel(b_ref, c_ref, h_ref, ch_ref, hh_ref, w_ref, o_ref, xs_ref, *, tm):
    first = pl.program_id(1) == 0
    halo = ch_ref[...] * hh_ref[...]
    xs_ref[0:HALO, :] = jnp.where(first, 0.0, halo)
    xs_ref[HALO:HALO + tm, :] = c_ref[...] * h_ref[...]
    conv = jnp.zeros((tm, SC_WIDTH), F32)
    for i in range(SC_CONV):
        conv = conv + w_ref[i:i + 1, :] * xs_ref[pl.ds(HALO - SC_CONV + 1 + i, tm), :]
    o_ref[...] = (b_ref[...] * conv).astype(o_ref.dtype)


def _short_conv(proj, conv_w, batch, seq, *, tm):
    tm = min(tm, seq)
    nt = seq // tm
    hal = _halo_index(tm)
    tile = lambda cb: pl.BlockSpec((tm, SC_WIDTH), lambda b, i: (b * nt + i, cb))
    halo = lambda cb: pl.BlockSpec((HALO, SC_WIDTH), lambda b, i: (hal(b * nt + i), cb))
    return pl.pallas_call(
        functools.partial(_sconv_kernel, tm=tm),
        grid=(batch, nt),
        in_specs=[tile(0), tile(1), tile(2), halo(1), halo(2),
                  pl.BlockSpec((SC_CONV, SC_WIDTH), lambda b, i: (0, 0))],
        out_specs=pl.BlockSpec((tm, SC_WIDTH), lambda b, i: (b * nt + i, 0)),
        out_shape=jax.ShapeDtypeStruct((batch * seq, SC_WIDTH), BF16),
        scratch_shapes=[pltpu.VMEM((HALO + tm, SC_WIDTH), F32)],
        compiler_params=_cparams("parallel", "arbitrary"),
        name="short_conv",
    )(proj, proj, proj, proj, proj, conv_w)


def _gdn_kernel(q_ref, k_ref, v_ref, z_ref, qh_ref, kh_ref, vh_ref, g_ref,
                cwq_ref, cwk_ref, cwv_ref, alog_ref, dtb_ref, nw_ref,
                o_ref, state_ref, xs_ref):
    c = GDN_CHUNK
    n = pl.program_id(1)
    first = n == 0

    @pl.when(first)
    def _():
        state_ref[...] = jnp.zeros_like(state_ref)

    def conv_silu(x_ref, halo_ref, w_ref):
        xs_ref[0:HALO, :] = jnp.where(first, 0.0, halo_ref[...])
        xs_ref[HALO:HALO + c, :] = x_ref[...]
        y = jnp.zeros((c, GDN_WIDTH), F32)
        for i in range(GDN_CONV):
            y = y + w_ref[i:i + 1, :] * xs_ref[pl.ds(HALO - GDN_CONV + 1 + i, c), :]
        return y * jax.nn.sigmoid(y)

    q_all = conv_silu(q_ref, qh_ref, cwq_ref)
    k_all = conv_silu(k_ref, kh_ref, cwk_ref)
    v_all = conv_silu(v_ref, vh_ref, cwv_ref)

    gates = g_ref[...]
    g_all = -jnp.exp(alog_ref[...]) * _softplus(gates + dtb_ref[...])
    beta_all = jax.nn.sigmoid(gates)

    row = lax.broadcasted_iota(jnp.int32, (c, c), 0)
    col = lax.broadcasted_iota(jnp.int32, (c, c), 1)
    tri = col <= row
    strict = col < row
    lower_ones = jnp.where(tri, 1.0, 0.0)
    eye = jnp.where(row == col, 1.0, 0.0)
    gcum_col = _dot(lower_ones, g_all)
    gcum_row = _dot_nt(g_all.T, lower_ones)

    for h in range(GDN_HEADS):
        sl = slice(h * HEAD_DIM, (h + 1) * HEAD_DIM)
        q = q_all[:, sl]
        k = k_all[:, sl]
        v = v_all[:, sl]
        q = q * lax.rsqrt(jnp.sum(q * q, axis=-1, keepdims=True) + L2_EPS) * (HEAD_DIM ** -0.5)
        k = k * lax.rsqrt(jnp.sum(k * k, axis=-1, keepdims=True) + L2_EPS)
        g_c = gcum_col[:, h:h + 1]
        g_r = gcum_row[h:h + 1, :]
        beta = beta_all[:, GDN_HEADS + h:GDN_HEADS + h + 1]
        g_last = g_c[c - 1:c, :]
        decay = jnp.where(tri, jnp.exp(jnp.where(tri, g_c - g_r, 0.0)), 0.0)
        kb16 = k.astype(BF16)
        kk = _dot_nt(kb16, kb16)
        qk = _dot_nt(q.astype(BF16), kb16)
        m = jnp.where(strict, beta * kk * decay, 0.0)
        attn = jnp.where(tri, qk * decay, 0.0)
        p = -m
        t_mat = eye + p
        for _ in range(5):
            p = _dot(p, p)
            t_mat = t_mat + _dot(t_mat, p)
        e_g = jnp.exp(g_c)
        rhs = jnp.concatenate([v * beta, k * (beta * e_g)], axis=1)
        uw = _dot(t_mat.astype(BF16), rhs.astype(BF16))
        u = uw[:, :HEAD_DIM]
        w = uw[:, HEAD_DIM:]
        q_dec = q * e_g
        k_dec = k * jnp.exp(g_last - g_c)
        state = state_ref[h]
        s16 = state.astype(BF16)
        v_new = u - _dot(w.astype(BF16), s16)
        v16 = v_new.astype(BF16)
        o = _dot(q_dec.astype(BF16), s16) + _dot(attn.astype(BF16), v16)
        state_ref[h] = state * jnp.exp(g_last) + _dot_tn(k_dec.astype(BF16), v16)
        r = lax.rsqrt(jnp.mean(o * o, axis=-1, keepdims=True) + RMS_EPS)
        zz = z_ref[:, sl]
        y = (o * r * nw_ref[...]) * (zz * jax.nn.sigmoid(zz))
        o_ref[:, sl] = y.astype(o_ref.dtype)


def _gdn_mixer(proj, gates, conv_w, a_log, dt_bias, norm_w, batch, seq, col0):
    c = GDN_CHUNK
    nc = seq // c
    cb0 = col0 // GDN_WIDTH
    hal = _halo_index(c)
    tile = lambda cb: pl.BlockSpec((c, GDN_WIDTH), lambda b, i: (b * nc + i, cb0 + cb))
    halo = lambda cb: pl.BlockSpec((HALO, GDN_WIDTH), lambda b, i: (hal(b * nc + i), cb0 + cb))
    cw = lambda cb: pl.BlockSpec((GDN_CONV, GDN_WIDTH), lambda b, i: (0, cb))
    rowvec = pl.BlockSpec((1, HEAD_DIM), lambda b, i: (0, 0))
    pad = lambda a: jnp.zeros((1, HEAD_DIM), F32).at[0, :GDN_HEADS].set(a)
    return pl.pallas_call(
        _gdn_kernel,
        grid=(batch, nc),
        in_specs=[tile(0), tile(1), tile(2), tile(3), halo(0), halo(1), halo(2),
                  pl.BlockSpec((c, HEAD_DIM), lambda b, i: (b * nc + i, 0)),
                  cw(0), cw(1), cw(2), rowvec, rowvec, rowvec],
        out_specs=pl.BlockSpec((c, GDN_WIDTH), lambda b, i: (b * nc + i, 0)),
        out_shape=jax.ShapeDtypeStruct((batch * seq, GDN_WIDTH), BF16),
        scratch_shapes=[pltpu.VMEM((GDN_HEADS, HEAD_DIM, HEAD_DIM), F32),
                        pltpu.VMEM((HALO + c, GDN_WIDTH), F32)],
        compiler_params=_cparams("parallel", "arbitrary"),
        name="gdn_mixer",
    )(proj, proj, proj, proj, proj, proj, proj, gates,
      conv_w, conv_w, conv_w, pad(a_log), pad(dt_bias), norm_w.reshape(1, HEAD_DIM))


def _out_proj_kernel(ysb_ref, ysc_ref, ygdn_ref, w_ref, res_ref, o_ref):
    acc = res_ref[...]
    acc = acc + _dot(ysb_ref[...], w_ref[0:SB_WIDTH, :])
    acc = acc + _dot(ysc_ref[...], w_ref[SB_WIDTH:SB_WIDTH + SC_WIDTH, :])
    acc = acc + _dot(ygdn_ref[...], w_ref[SB_WIDTH + SC_WIDTH:, :])
    o_ref[...] = acc


def _out_proj(y_sb, y_sc, y_gdn, w, res, *, tm, tn):
    m, d = res.shape
    kdim = w.shape[0]
    tm, tn = min(tm, m), min(tn, d)
    rows = lambda width: pl.BlockSpec((tm, width), lambda i, j: (i, 0))
    return pl.pallas_call(
        _out_proj_kernel,
        grid=(m // tm, d // tn),
        in_specs=[rows(SB_WIDTH), rows(SC_WIDTH), rows(GDN_WIDTH),
                  pl.BlockSpec((kdim, tn), lambda i, j: (0, j)),
                  pl.BlockSpec((tm, tn), lambda i, j: (i, j))],
        out_specs=pl.BlockSpec((tm, tn), lambda i, j: (i, j)),
        out_shape=jax.ShapeDtypeStruct((m, d), F32),
        compiler_params=_cparams("parallel", "parallel"),
        name="out_proj",
    )(y_sb, y_sc, y_gdn, w, res)


def _matmul_res_kernel(a_ref, w_ref, res_ref, o_ref):
    @pl.when(pl.program_id(2) == 0)
    def _():
        o_ref[...] = res_ref[...]

    o_ref[...] += _dot(a_ref[...], w_ref[...])


def _matmul_res(a, w, res, *, tm, tn, tk):
    m, kdim = a.shape
    n = w.shape[1]
    tm, tn, tk = min(tm, m), min(tn, n), min(tk, kdim)
    return pl.pallas_call(
        _matmul_res_kernel,
        grid=(m // tm, n // tn, kdim // tk),
        in_specs=[pl.BlockSpec((tm, tk), lambda i, j, k: (i, k)),
                  pl.BlockSpec((tk, tn), lambda i, j, k: (k, j)),
                  pl.BlockSpec((tm, tn), lambda i, j, k: (i, j))],
        out_specs=pl.BlockSpec((tm, tn), lambda i, j, k: (i, j)),
        out_shape=jax.ShapeDtypeStruct((m, n), F32),
        compiler_params=_cparams("parallel", "parallel", "arbitrary"),
        name="matmul_res",
    )(a, w, res)


def _rmsnorm_kernel(x_ref, nw_ref, o_ref):
    x = x_ref[...]
    r = lax.rsqrt(jnp.mean(x * x, axis=-1, keepdims=True) + RMS_EPS)
    o_ref[...] = x * r * nw_ref[...]


def _rmsnorm(x, norm_w, *, tm):
    m, d = x.shape
    tm = min(tm, m)
    return pl.pallas_call(
        _rmsnorm_kernel,
        grid=(m // tm,),
        in_specs=[pl.BlockSpec((tm, d), lambda i: (i, 0)),
                  pl.BlockSpec((1, d), lambda i: (0, 0))],
        out_specs=pl.BlockSpec((tm, d), lambda i: (i, 0)),
        out_shape=jax.ShapeDtypeStruct((m, d), F32),
        compiler_params=_cparams("parallel"),
        name="final_rmsnorm",
    )(x, norm_w.reshape(1, d))


def _layer(x, batch, seq, norm1_w, w_in, sc_conv_w, gdn_conv_w, gdn_a_log, gdn_dt_bias, gdn_norm_w,
           w_out, norm2_w, w_up, w_down):
    d = x.shape[1]
    sb_cols = 3 * SB_WIDTH
    main_cols = 3 * SC_WIDTH + 4 * GDN_WIDTH
    w_sb = w_in[:, :sb_cols].astype(BF16)
    w_main = w_in[:, sb_cols:sb_cols + main_cols].astype(BF16)
    w_gate = jnp.zeros((d, HEAD_DIM), BF16).at[:, :2 * GDN_HEADS].set(
        w_in[:, sb_cols + main_cols:].astype(BF16))

    qkv_sb = _norm_matmul(x, norm1_w, w_sb, BF16, tm=1024, tn=1152)
    proj = _norm_matmul(x, norm1_w, w_main, F32, tm=1024, tn=768)
    gates = _norm_matmul(x, norm1_w, w_gate, F32, tm=1024, tn=HEAD_DIM)

    y_sb = _sb_attention(qkv_sb, batch, seq, blk=256)
    y_sc = _short_conv(proj, sc_conv_w, batch, seq, tm=512)
    y_gdn = _gdn_mixer(proj, gates, gdn_conv_w, gdn_a_log, gdn_dt_bias, gdn_norm_w,
                       batch, seq, 3 * SC_WIDTH)
    x = _out_proj(y_sb, y_sc, y_gdn, w_out.astype(BF16), x, tm=1024, tn=1024)
    hidden = _norm_matmul(x, norm2_w, w_up.astype(BF16), BF16, tm=1024, tn=1024, sq_relu=True)
    return _matmul_res(hidden, w_down.astype(BF16), x, tm=1024, tn=1024, tk=2048)


def kernel(x, norm1_w, w_in, sc_conv_w, gdn_conv_w, gdn_a_log, gdn_dt_bias, gdn_norm_w, w_out, norm2_w,
           w_up, w_down, final_norm_w):
    batch, seq, d = x.shape
    h = x.reshape(batch * seq, d)
    for l in range(norm1_w.shape[0]):
        h = _layer(h, batch, seq, norm1_w[l], w_in[l], sc_conv_w[l], gdn_conv_w[l], gdn_a_log[l],
                   gdn_dt_bias[l], gdn_norm_w[l], w_out[l], norm2_w[l], w_up[l], w_down[l])
    return _rmsnorm(h, final_norm_w, tm=512).reshape(batch, seq, d)
```

```python
import functools

import jax
import jax.numpy as jnp
from jax import lax
from jax.experimental import pallas as pl
from jax.experimental.pallas import tpu as pltpu

HEAD_DIM = 128
SB_HEADS = 6
SB_WIDTH = SB_HEADS * HEAD_DIM
SC_WIDTH = 4 * HEAD_DIM
GDN_HEADS = 6
GDN_WIDTH = GDN_HEADS * HEAD_DIM
SC_CONV = 3
GDN_CONV = 4
GDN_CHUNK = 64
RMS_EPS = 1e-6
L2_EPS = 1e-6
HALO = 8
VMEM_LIMIT_BYTES = 56 * 1024 * 1024

F32 = jnp.float32
BF16 = jnp.bfloat16


def _cparams(*sem):
    return pltpu.CompilerParams(dimension_semantics=sem, vmem_limit_bytes=VMEM_LIMIT_BYTES)


def _dot(a, b):
    return jnp.dot(a, b, preferred_element_type=F32)


def _dot_nt(a, b):
    return lax.dot_general(a, b, (((1,), (1,)), ((), ())), preferred_element_type=F32)


def _dot_tn(a, b):
    return lax.dot_general(a, b, (((0,), (0,)), ((), ())), preferred_element_type=F32)


def _norm_matmul_kernel(x_ref, nw_ref, w_ref, o_ref, hn_ref, *, sq_relu, valid_cols):
    @pl.when(pl.program_id(1) == 0)
    def _():
        x = x_ref[...]
        r = lax.rsqrt(jnp.mean(x * x, axis=-1, keepdims=True) + RMS_EPS)
        hn_ref[...] = (x * r * nw_ref[...]).astype(BF16)

    w = w_ref[...]
    if valid_cols is not None:
        lane = lax.broadcasted_iota(jnp.int32, w.shape, 1)
        w = jnp.where(lane < valid_cols, w, jnp.zeros_like(w))
    acc = _dot(hn_ref[...], w)
    if sq_relu:
        acc = jnp.square(jnp.maximum(acc, 0.0))
    o_ref[...] = acc.astype(o_ref.dtype)


def _norm_matmul(x, norm_w, w, layer, col0, n, out_dtype, *, tm, tn, sq_relu=False):
    m, d = x.shape
    tm, tn = min(tm, m), min(tn, n)
    assert col0 % tn == 0 and n % tn == 0 and m % tm == 0
    cb0 = col0 // tn
    over = col0 + n - w.shape[2]
    return pl.pallas_call(
        functools.partial(_norm_matmul_kernel, sq_relu=sq_relu, valid_cols=n - over if over > 0 else None),
        grid=(m // tm, n // tn),
        in_specs=[pl.BlockSpec((tm, d), lambda i, j: (i, 0)),
                  pl.BlockSpec((None, 1, d), lambda i, j: (layer, 0, 0)),
                  pl.BlockSpec((None, d, tn), lambda i, j: (layer, 0, cb0 + j))],
        out_specs=pl.BlockSpec((tm, tn), lambda i, j: (i, j)),
        out_shape=jax.ShapeDtypeStruct((m, n), out_dtype),
        scratch_shapes=[pltpu.VMEM((tm, d), BF16)],
        compiler_params=_cparams("parallel", "arbitrary"),
        name="norm_matmul",
    )(x, norm_w.reshape(norm_w.shape[0], 1, d), w)


def _softplus(z):
    return jnp.maximum(z, 0.0) + jnp.log(1.0 + jnp.exp(-jnp.abs(z)))


SB_ZERO_MASS = 110.0


def _attn_kernel(q_ref, k_ref, v_ref, o_ref, *, blk, scale):
    i = pl.program_id(2)
    q = q_ref[...]
    row = lax.broadcasted_iota(jnp.int32, (blk, blk), 0)
    col = lax.broadcasted_iota(jnp.int32, (blk, blk), 1)
    suffix_ones = jnp.where(row >= col, 1.0, 0.0).astype(BF16)

    def suffix_sum(sp):
        hi = sp.astype(BF16)
        lo = (sp - hi.astype(F32)).astype(BF16)
        return _dot(hi, suffix_ones) + _dot(lo, suffix_ones)

    def tile(start, width, carry, acc, masked):
        kb = k_ref[pl.ds(start, width), :]
        vb = v_ref[pl.ds(start, width), :]
        z = _dot_nt(q, kb) * scale
        sp = _softplus(z)
        if masked:
            q_pos = i * blk + lax.broadcasted_iota(jnp.int32, (blk, width), 0)
            k_pos = start + lax.broadcasted_iota(jnp.int32, (blk, width), 1)
            causal = k_pos < q_pos
            sp = jnp.where(causal, sp, 0.0)
        rems = []
        for part in reversed(range(width // blk)):
            rem = suffix_sum(sp[:, part * blk:(part + 1) * blk]) + carry
            carry = rem[:, 0:1]
            rems.append(rem)
        rem = rems[0] if len(rems) == 1 else jnp.concatenate(rems[::-1], axis=1)
        w = jnp.exp(z - rem)
        if masked:
            w = jnp.where(causal, w, 0.0)
        return carry, acc + _dot(w.astype(BF16), vb)

    start0 = pl.multiple_of(jnp.maximum(i - 1, 0) * blk, blk)
    carry, acc = tile(start0, 2 * blk, jnp.zeros((blk, 1), F32), jnp.zeros((blk, HEAD_DIM), F32), True)
    n_left = jnp.maximum(i - 1, 0)

    def cond(c):
        return jnp.logical_and(c[0] < n_left // 2, c[1] <= SB_ZERO_MASS)

    def body(c):
        p = c[0]
        start = pl.multiple_of(start0 - (p + 1) * 2 * blk, blk)
        carry, acc = tile(start, 2 * blk, c[2], c[3], False)
        return p + 1, jnp.min(carry), carry, acc

    _, min_mass, carry, acc = lax.while_loop(cond, body, (jnp.int32(0), jnp.min(carry), carry, acc))

    def last(c):
        return tile(0, blk, c[0], c[1], False)

    odd_left = jnp.logical_and(n_left % 2 == 1, min_mass <= SB_ZERO_MASS)
    carry, acc = lax.cond(odd_left, last, lambda c: c, (carry, acc))
    o_ref[...] = acc.astype(o_ref.dtype)


def _sb_attention(qkv, batch, seq, *, blk):
    assert seq % (2 * blk) == 0
    nq = seq // blk
    h = SB_HEADS
    return pl.pallas_call(
        functools.partial(_attn_kernel, blk=blk, scale=HEAD_DIM ** -0.5),
        grid=(batch, h, nq),
        in_specs=[pl.BlockSpec((blk, HEAD_DIM), lambda b, hh, i: (b * nq + i, hh)),
                  pl.BlockSpec((seq, HEAD_DIM), lambda b, hh, i: (b, h + hh)),
                  pl.BlockSpec((seq, HEAD_DIM), lambda b, hh, i: (b, 2 * h + hh))],
        out_specs=pl.BlockSpec((blk, HEAD_DIM), lambda b, hh, i: (b * nq + i, hh)),
        out_shape=jax.ShapeDtypeStruct((batch * seq, SB_WIDTH), BF16),
        compiler_params=_cparams("parallel", "parallel", "arbitrary"),
        name="sb_attention",
    )(qkv, qkv, qkv)


def _halo_index(rows_per_tile):
    per = rows_per_tile // HALO
    return lambda t: jnp.maximum(t * per - 1, 0)


def _sconv_kernel(b_ref, c_ref, h_ref, ch_ref, hh_ref, w_ref, o_ref, xs_ref, *, tm):
    first = pl.program_id(1) == 0
    halo = ch_ref[...] * hh_ref[...]
    xs_ref[0:HALO, :] = jnp.where(first, 0.0, halo)
    xs_ref[HALO:HALO + tm, :] = c_ref[...] * h_ref[...]
    conv = jnp.zeros((tm, SC_WIDTH), F32)
    for i in range(SC_CONV):
        conv = conv + w_ref[i:i + 1, :] * xs_ref[pl.ds(HALO - SC_CONV + 1 + i, tm), :]
    o_ref[...] = (b_ref[...] * conv).astype(o_ref.dtype)


def _short_conv(proj, conv_w, layer, batch, seq, *, tm):
    tm = min(tm, seq)
    nt = seq // tm
    hal = _halo_index(tm)
    tile = lambda cb: pl.BlockSpec((tm, SC_WIDTH), lambda b, i: (b * nt + i, cb))
    halo = lambda cb: pl.BlockSpec((HALO, SC_WIDTH), lambda b, i: (hal(b * nt + i), cb))
    return pl.pallas_call(
        functools.partial(_sconv_kernel, tm=tm),
        grid=(batch, nt),
        in_specs=[tile(0), tile(1), tile(2), halo(1), halo(2),
                  pl.BlockSpec((None, SC_CONV, SC_WIDTH), lambda b, i: (layer, 0, 0))],
        out_specs=pl.BlockSpec((tm, SC_WIDTH), lambda b, i: (b * nt + i, 0)),
        out_shape=jax.ShapeDtypeStruct((batch * seq, SC_WIDTH), BF16),
        scratch_shapes=[pltpu.VMEM((HALO + tm, SC_WIDTH), F32)],
        compiler_params=_cparams("parallel", "arbitrary"),
        name="short_conv",
    )(proj, proj, proj, proj, proj, conv_w)


def _gdn_kernel(q_ref, k_ref, v_ref, z_ref, qh_ref, kh_ref, vh_ref, g_ref,
                cwq_ref, cwk_ref, cwv_ref, alog_ref, dtb_ref, nw_ref,
                o_ref, state_ref, xs_ref, *, chunks):
    c = GDN_CHUNK
    rows = chunks * c
    first = pl.program_id(1) == 0

    @pl.when(first)
    def _():
        state_ref[...] = jnp.zeros_like(state_ref)

    def conv_silu(slot, x_ref, halo_ref, w_ref):
        xs_ref[slot, 0:HALO, :] = jnp.where(first, 0.0, halo_ref[...])
        xs_ref[slot, HALO:HALO + rows, :] = x_ref[...]
        y = jnp.zeros((rows, GDN_WIDTH), F32)
        for i in range(GDN_CONV):
            y = y + w_ref[i:i + 1, :] * xs_ref[slot, pl.ds(HALO - GDN_CONV + 1 + i, rows), :]
        return y * jax.nn.sigmoid(y)

    q_all = conv_silu(0, q_ref, qh_ref, cwq_ref)
    k_all = conv_silu(1, k_ref, kh_ref, cwk_ref)
    v_all = conv_silu(2, v_ref, vh_ref, cwv_ref)

    gates = g_ref[...]
    g_all = -jnp.exp(alog_ref[...]) * _softplus(gates + dtb_ref[...])
    beta_all = jax.nn.sigmoid(gates)

    row = lax.broadcasted_iota(jnp.int32, (c, c), 0)
    col = lax.broadcasted_iota(jnp.int32, (c, c), 1)
    tri = col <= row
    strict = col < row
    lower_ones = jnp.where(tri, 1.0, 0.0)
    eye = jnp.where(row == col, 1.0, 0.0)

    heads = range(GDN_HEADS)
    pairs = [(ch, h) for ch in range(chunks) for h in heads]
    rs = lambda ch: slice(ch * c, (ch + 1) * c)
    hs = lambda h: slice(h * HEAD_DIM, (h + 1) * HEAD_DIM)

    gcum_col = [_dot(lower_ones, g_all[rs(ch)]) for ch in range(chunks)]
    gcum_row = [_dot_nt(g_all[rs(ch)].T, lower_ones) for ch in range(chunks)]

    q, k, v, beta, g_c, g_last, e_g, decay = {}, {}, {}, {}, {}, {}, {}, {}
    for p in pairs:
        ch, h = p
        qq = q_all[rs(ch), hs(h)]
        kk = k_all[rs(ch), hs(h)]
        q[p] = qq * lax.rsqrt(jnp.sum(qq * qq, axis=-1, keepdims=True) + L2_EPS) * (HEAD_DIM ** -0.5)
        k[p] = kk * lax.rsqrt(jnp.sum(kk * kk, axis=-1, keepdims=True) + L2_EPS)
        v[p] = v_all[rs(ch), hs(h)]
        beta[p] = beta_all[rs(ch), GDN_HEADS + h:GDN_HEADS + h + 1]
        g_c[p] = gcum_col[ch][:, h:h + 1]
        g_r = gcum_row[ch][h:h + 1, :]
        g_last[p] = g_c[p][c - 1:c, :]
        e_g[p] = jnp.exp(g_c[p])
        decay[p] = jnp.where(tri, jnp.exp(jnp.where(tri, g_c[p] - g_r, 0.0)), 0.0)

    k16 = {p: k[p].astype(BF16) for p in pairs}
    kk = {p: _dot_nt(k16[p], k16[p]) for p in pairs}
    qk = {p: _dot_nt(q[p].astype(BF16), k16[p]) for p in pairs}
    attn = {p: jnp.where(tri, qk[p] * decay[p], 0.0).astype(BF16) for p in pairs}
    pw = {p: jnp.where(strict, -(beta[p] * kk[p] * decay[p]), 0.0) for p in pairs}
    t_mat = {p: eye + pw[p] for p in pairs}
    for _ in range(5):
        pw = {p: _dot(pw[p], pw[p]) for p in pairs}
        t_mat = {p: t_mat[p] + _dot(t_mat[p], pw[p]) for p in pairs}
    rhs = {p: jnp.concatenate([v[p] * beta[p], k[p] * (beta[p] * e_g[p])], axis=1).astype(BF16) for p in pairs}
    uw = {p: _dot(t_mat[p].astype(BF16), rhs[p]) for p in pairs}
    wq = {p: jnp.concatenate([uw[p][:, HEAD_DIM:], q[p] * e_g[p]], axis=0).astype(BF16) for p in pairs}
    k_dec = {p: (k[p] * jnp.exp(g_last[p] - g_c[p])).astype(BF16) for p in pairs}

    state = {h: state_ref[h] for h in heads}
    out = {}
    for ch in range(chunks):
        s16 = {h: state[h].astype(BF16) for h in heads}
        ws = {h: _dot(wq[ch, h], s16[h]) for h in heads}
        v_new = {h: (uw[ch, h][:, :HEAD_DIM] - ws[h][:c]).astype(BF16) for h in heads}
        out.update({(ch, h): ws[h][c:] + _dot(attn[ch, h], v_new[h]) for h in heads})
        state = {h: state[h] * jnp.exp(g_last[ch, h]) + _dot_tn(k_dec[ch, h], v_new[h]) for h in heads}
    for h in heads:
        state_ref[h] = state[h]

    for p in pairs:
        ch, h = p
        o = out[p]
        r = lax.rsqrt(jnp.mean(o * o, axis=-1, keepdims=True) + RMS_EPS)
        zz = z_ref[rs(ch), hs(h)]
        y = (o * r * nw_ref[...]) * (zz * jax.nn.sigmoid(zz))
        o_ref[rs(ch), hs(h)] = y.astype(o_ref.dtype)


def _gdn_mixer(proj, gates, conv_w, a_log, dt_bias, norm_w, layer, batch, seq, col0, *, chunks_per_step):
    rows = chunks_per_step * GDN_CHUNK
    assert seq % rows == 0
    nt = seq // rows
    cb0 = col0 // GDN_WIDTH
    hal = _halo_index(rows)
    tile = lambda cb: pl.BlockSpec((rows, GDN_WIDTH), lambda b, i: (b * nt + i, cb0 + cb))
    halo = lambda cb: pl.BlockSpec((HALO, GDN_WIDTH), lambda b, i: (hal(b * nt + i), cb0 + cb))
    cw = lambda cb: pl.BlockSpec((None, GDN_CONV, GDN_WIDTH), lambda b, i: (layer, 0, cb))
    rowvec = pl.BlockSpec((None, 1, HEAD_DIM), lambda b, i: (layer, 0, 0))
    return pl.pallas_call(
        functools.partial(_gdn_kernel, chunks=chunks_per_step),
        grid=(batch, nt),
        in_specs=[tile(0), tile(1), tile(2), tile(3), halo(0), halo(1), halo(2),
                  pl.BlockSpec((rows, HEAD_DIM), lambda b, i: (b * nt + i, 0)),
                  cw(0), cw(1), cw(2), rowvec, rowvec, rowvec],
        out_specs=pl.BlockSpec((rows, GDN_WIDTH), lambda b, i: (b * nt + i, 0)),
        out_shape=jax.ShapeDtypeStruct((batch * seq, GDN_WIDTH), BF16),
        scratch_shapes=[pltpu.VMEM((GDN_HEADS, HEAD_DIM, HEAD_DIM), F32),
                        pltpu.VMEM((3, HALO + rows, GDN_WIDTH), F32)],
        compiler_params=_cparams("parallel", "arbitrary"),
        name="gdn_mixer",
    )(proj, proj, proj, proj, proj, proj, proj, gates,
      conv_w, conv_w, conv_w, a_log, dt_bias, norm_w)


def _out_proj_kernel(ysb_ref, ysc_ref, ygdn_ref, w_ref, res_ref, o_ref):
    acc = res_ref[...]
    acc = acc + _dot(ysb_ref[...], w_ref[0:SB_WIDTH, :])
    acc = acc + _dot(ysc_ref[...], w_ref[SB_WIDTH:SB_WIDTH + SC_WIDTH, :])
    acc = acc + _dot(ygdn_ref[...], w_ref[SB_WIDTH + SC_WIDTH:, :])
    o_ref[...] = acc


def _out_proj(y_sb, y_sc, y_gdn, w, layer, res, *, tm, tn):
    m, d = res.shape
    kdim = w.shape[1]
    tm, tn = min(tm, m), min(tn, d)
    rows = lambda width: pl.BlockSpec((tm, width), lambda i, j: (i, 0))
    return pl.pallas_call(
        _out_proj_kernel,
        grid=(m // tm, d // tn),
        in_specs=[rows(SB_WIDTH), rows(SC_WIDTH), rows(GDN_WIDTH),
                  pl.BlockSpec((None, kdim, tn), lambda i, j: (layer, 0, j)),
                  pl.BlockSpec((tm, tn), lambda i, j: (i, j))],
        out_specs=pl.BlockSpec((tm, tn), lambda i, j: (i, j)),
        out_shape=jax.ShapeDtypeStruct((m, d), F32),
        compiler_params=_cparams("parallel", "parallel"),
        name="out_proj",
    )(y_sb, y_sc, y_gdn, w, res)


def _matmul_res_kernel(a_ref, w_ref, res_ref, o_ref):
    @pl.when(pl.program_id(2) == 0)
    def _():
        o_ref[...] = res_ref[...]

    o_ref[...] += _dot(a_ref[...], w_ref[...])


def _matmul_res(a, w, layer, res, *, tm, tn, tk):
    m, kdim = a.shape
    n = w.shape[2]
    tm, tn, tk = min(tm, m), min(tn, n), min(tk, kdim)
    return pl.pallas_call(
        _matmul_res_kernel,
        grid=(m // tm, n // tn, kdim // tk),
        in_specs=[pl.BlockSpec((tm, tk), lambda i, j, k: (i, k)),
                  pl.BlockSpec((None, tk, tn), lambda i, j, k: (layer, k, j)),
                  pl.BlockSpec((tm, tn), lambda i, j, k: (i, j))],
        out_specs=pl.BlockSpec((tm, tn), lambda i, j, k: (i, j)),
        out_shape=jax.ShapeDtypeStruct((m, n), F32),
        compiler_params=_cparams("parallel", "parallel", "arbitrary"),
        name="matmul_res",
    )(a, w, res)


def _rmsnorm_kernel(x_ref, nw_ref, o_ref):
    x = x_ref[...]
    r = lax.rsqrt(jnp.mean(x * x, axis=-1, keepdims=True) + RMS_EPS)
    o_ref[...] = x * r * nw_ref[...]


def _rmsnorm(x, norm_w, *, tm):
    m, d = x.shape
    tm = min(tm, m)
    return pl.pallas_call(
        _rmsnorm_kernel,
        grid=(m // tm,),
        in_specs=[pl.BlockSpec((tm, d), lambda i: (i, 0)),
                  pl.BlockSpec((1, d), lambda i: (0, 0))],
        out_specs=pl.BlockSpec((tm, d), lambda i: (i, 0)),
        out_shape=jax.ShapeDtypeStruct((m, d), F32),
        compiler_params=_cparams("parallel"),
        name="final_rmsnorm",
    )(x, norm_w.reshape(1, d))


def kernel(x, norm1_w, w_in, sc_conv_w, gdn_conv_w, gdn_a_log, gdn_dt_bias, gdn_norm_w, w_out, norm2_w,
           w_up, w_down, final_norm_w):
    batch, seq, d = x.shape
    layers = norm1_w.shape[0]
    sb_cols = 3 * SB_WIDTH
    main_cols = 3 * SC_WIDTH + 4 * GDN_WIDTH
    w_in16, w_out16, w_up16, w_down16 = (w.astype(BF16) for w in (w_in, w_out, w_up, w_down))
    pad_heads = lambda a: jnp.zeros((layers, 1, HEAD_DIM), F32).at[:, 0, :GDN_HEADS].set(a)
    a_log, dt_bias = pad_heads(gdn_a_log), pad_heads(gdn_dt_bias)
    gdn_nw = gdn_norm_w.reshape(layers, 1, HEAD_DIM)

    h = x.reshape(batch * seq, d)
    for l in range(layers):
        qkv_sb = _norm_matmul(h, norm1_w, w_in16, l, 0, sb_cols, BF16, tm=1024, tn=1152)
        proj = _norm_matmul(h, norm1_w, w_in16, l, sb_cols, main_cols, F32, tm=1024, tn=768)
        gates = _norm_matmul(h, norm1_w, w_in16, l, sb_cols + main_cols, HEAD_DIM, F32, tm=1024, tn=HEAD_DIM)
        y_sb = _sb_attention(qkv_sb, batch, seq, blk=256)
        y_sc = _short_conv(proj, sc_conv_w, l, batch, seq, tm=512)
        y_gdn = _gdn_mixer(proj, gates, gdn_conv_w, a_log, dt_bias, gdn_nw, l, batch, seq, 3 * SC_WIDTH,
                           chunks_per_step=2)
        h = _out_proj(y_sb, y_sc, y_gdn, w_out16, l, h, tm=1024, tn=1024)
        hidden = _norm_matmul(h, norm2_w, w_up16, l, 0, w_up.shape[2], BF16, tm=1024, tn=1024, sq_relu=True)
        h = _matmul_res(hidden, w_down16, l, h, tm=1024, tn=1024, tk=2048)
    return _rmsnorm(h, final_norm_w, tm=512).reshape(batch, seq, d)
```

```python
import functools

import jax
import jax.numpy as jnp
from jax import lax
from jax.experimental import pallas as pl
from jax.experimental.pallas import tpu as pltpu

HEAD_DIM = 128
SB_HEADS = 6
SB_WIDTH = SB_HEADS * HEAD_DIM
SC_WIDTH = 4 * HEAD_DIM
GDN_HEADS = 6
GDN_WIDTH = GDN_HEADS * HEAD_DIM
SC_CONV = 3
GDN_CONV = 4
GDN_CHUNK = 64
RMS_EPS = 1e-6
L2_EPS = 1e-6
HALO = 8
VMEM_LIMIT_BYTES = 56 * 1024 * 1024

F32 = jnp.float32
BF16 = jnp.bfloat16


def _cparams(*sem):
    return pltpu.CompilerParams(dimension_semantics=sem, vmem_limit_bytes=VMEM_LIMIT_BYTES)


def _dot(a, b):
    return jnp.dot(a, b, preferred_element_type=F32)


def _dot_nt(a, b):
    return lax.dot_general(a, b, (((1,), (1,)), ((), ())), preferred_element_type=F32)


def _dot_tn(a, b):
    return lax.dot_general(a, b, (((0,), (0,)), ((), ())), preferred_element_type=F32)


def _rms_normalize(x_ref, nw_ref):
    x = x_ref[...]
    r = lax.rsqrt(jnp.mean(x * x, axis=-1, keepdims=True) + RMS_EPS)
    return (x * r * nw_ref[...]).astype(BF16)


def _norm_matmul_kernel(x_ref, nw_ref, w_ref, o_ref, hn_ref, *, sq_relu):
    @pl.when(pl.program_id(1) == 0)
    def _():
        hn_ref[...] = _rms_normalize(x_ref, nw_ref)

    acc = _dot(hn_ref[...], w_ref[...])
    if sq_relu:
        acc = jnp.square(jnp.maximum(acc, 0.0))
    o_ref[...] = acc.astype(o_ref.dtype)


def _norm_matmul(x, norm_w, w, layer, out_dtype, *, tm, tn, sq_relu=False):
    m, d = x.shape
    n = w.shape[2]
    tm, tn = min(tm, m), min(tn, n)
    assert n % tn == 0 and m % tm == 0
    return pl.pallas_call(
        functools.partial(_norm_matmul_kernel, sq_relu=sq_relu),
        grid=(m // tm, n // tn),
        in_specs=[pl.BlockSpec((tm, d), lambda i, j: (i, 0)),
                  pl.BlockSpec((None, 1, d), lambda i, j: (layer, 0, 0)),
                  pl.BlockSpec((None, d, tn), lambda i, j: (layer, 0, j))],
        out_specs=pl.BlockSpec((tm, tn), lambda i, j: (i, j)),
        out_shape=jax.ShapeDtypeStruct((m, n), out_dtype),
        scratch_shapes=[pltpu.VMEM((tm, d), BF16)],
        compiler_params=_cparams("parallel", "arbitrary"),
        name="norm_matmul",
    )(x, norm_w.reshape(norm_w.shape[0], 1, d), w)


IN_TILE = 768
SB_COLS = 3 * SB_WIDTH
PROJ_COLS = 3 * SC_WIDTH + 4 * GDN_WIDTH
IN_DIM = SB_COLS + PROJ_COLS + 2 * GDN_HEADS
IN_DIM_PADDED = SB_COLS + PROJ_COLS + HEAD_DIM
SB_TILES = SB_COLS // IN_TILE
PROJ_TILES = PROJ_COLS // IN_TILE


def _in_proj_kernel(x_ref, nw_ref, w_ref, sb_ref, proj_ref, gate_ref, hn_ref):
    j = pl.program_id(1)

    @pl.when(j == 0)
    def _():
        hn_ref[...] = _rms_normalize(x_ref, nw_ref)

    @pl.when(j < SB_TILES)
    def _():
        sb_ref[...] = _dot(hn_ref[...], w_ref[...]).astype(sb_ref.dtype)

    @pl.when(jnp.logical_and(j >= SB_TILES, j < SB_TILES + PROJ_TILES))
    def _():
        proj_ref[...] = _dot(hn_ref[...], w_ref[...])

    @pl.when(j == SB_TILES + PROJ_TILES)
    def _():
        gate_ref[...] = _dot(hn_ref[...], w_ref[:, :HEAD_DIM])


def _in_proj(x, norm_w, w_in16, layer, *, tm):
    m, d = x.shape
    tm = min(tm, m)
    assert m % tm == 0 and w_in16.shape[2] == IN_DIM_PADDED
    return pl.pallas_call(
        _in_proj_kernel,
        grid=(m // tm, SB_TILES + PROJ_TILES + 1),
        in_specs=[pl.BlockSpec((tm, d), lambda i, j: (i, 0)),
                  pl.BlockSpec((None, 1, d), lambda i, j: (layer, 0, 0)),
                  pl.BlockSpec((None, d, IN_TILE), lambda i, j: (layer, 0, j))],
        out_specs=[pl.BlockSpec((tm, IN_TILE), lambda i, j: (i, jnp.minimum(j, SB_TILES - 1))),
                   pl.BlockSpec((tm, IN_TILE), lambda i, j: (i, jnp.clip(j - SB_TILES, 0, PROJ_TILES - 1))),
                   pl.BlockSpec((tm, HEAD_DIM), lambda i, j: (i, 0))],
        out_shape=[jax.ShapeDtypeStruct((m, SB_COLS), BF16),
                   jax.ShapeDtypeStruct((m, PROJ_COLS), F32),
                   jax.ShapeDtypeStruct((m, HEAD_DIM), F32)],
        scratch_shapes=[pltpu.VMEM((tm, d), BF16)],
        compiler_params=_cparams("parallel", "arbitrary"),
        name="in_proj",
    )(x, norm_w.reshape(norm_w.shape[0], 1, d), w_in16)


def _cast_pad_kernel(w_ref, o_ref):
    n = w_ref.shape[-1]
    o_ref[:, :n] = w_ref[...].astype(o_ref.dtype)
    o_ref[:, n:] = jnp.zeros((o_ref.shape[0], o_ref.shape[1] - n), o_ref.dtype)


def _cast_pad_w_in(w_in, *, rows):
    layers, d, n = w_in.shape
    rows = min(rows, d)
    assert n == IN_DIM and d % rows == 0
    return pl.pallas_call(
        _cast_pad_kernel,
        grid=(layers, d // rows),
        in_specs=[pl.BlockSpec((None, rows, n), lambda l, i: (l, i, 0))],
        out_specs=pl.BlockSpec((None, rows, IN_DIM_PADDED), lambda l, i: (l, i, 0)),
        out_shape=jax.ShapeDtypeStruct((layers, d, IN_DIM_PADDED), BF16),
        compiler_params=_cparams("parallel", "parallel"),
        name="cast_pad_w_in",
    )(w_in)


def _softplus(z):
    return jnp.maximum(z, 0.0) + jnp.log(1.0 + jnp.exp(-jnp.abs(z)))


LOG2_E = 1.4426950408889634
SB_ZERO_MASS_LOG2 = 160.0
SB_ROW_GROUPS = 2


def _softplus_log2(z2):
    return jnp.maximum(z2, 0.0) + jnp.log(1.0 + jnp.exp2(-jnp.abs(z2))) * LOG2_E


def _attn_kernel(q_ref, k_ref, v_ref, o_ref, *, blk, scale):
    i = pl.program_id(2)
    q = q_ref[...]
    row = lax.broadcasted_iota(jnp.int32, (blk, blk), 0)
    col = lax.broadcasted_iota(jnp.int32, (blk, blk), 1)
    suffix_ones = jnp.where(row >= col, 1.0, 0.0).astype(BF16)
    causal = col < row

    def suffix_sum(sp):
        hi = sp.astype(BF16)
        lo = (sp - hi.astype(F32)).astype(BF16)
        return _dot(hi, suffix_ones) + _dot(lo, suffix_ones)

    def tile(k_blocks, v_blocks, carry, acc, diagonal_last, keep_first):
        kb = k_blocks[0] if len(k_blocks) == 1 else jnp.concatenate(k_blocks, axis=0)
        n = len(v_blocks)
        step = blk // SB_ROW_GROUPS
        groups = [slice(r * step, (r + 1) * step) for r in range(SB_ROW_GROUPS)]
        z2 = [_dot_nt(q[g], kb) * (scale * LOG2_E) for g in groups]
        sp = [_softplus_log2(z) for z in z2]
        carry = [carry[g] for g in groups]
        acc = [acc[g] for g in groups]
        for part in reversed(range(n)):
            cols = slice(part * blk, (part + 1) * blk)
            on_diagonal = diagonal_last and part == n - 1
            mass, w = [], []
            for r, g in enumerate(groups):
                sp_part = jnp.where(causal[g], sp[r][:, cols], 0.0) if on_diagonal else sp[r][:, cols]
                mass.append(suffix_sum(sp_part) + carry[r])
                carry[r] = mass[r][:, 0:1]
            for r, g in enumerate(groups):
                wr = jnp.exp2(z2[r][:, cols] - mass[r])
                w.append((jnp.where(causal[g], wr, 0.0) if on_diagonal else wr).astype(BF16))
            for r in range(SB_ROW_GROUPS):
                contrib = _dot(w[r], v_blocks[part])
                if part == 0 and keep_first is not None:
                    contrib = jnp.where(keep_first, contrib, 0.0)
                acc[r] = acc[r] + contrib
        return jnp.concatenate(carry, axis=0), jnp.concatenate(acc, axis=0)

    def key_block(ref, j):
        return ref[pl.ds(pl.multiple_of(j * blk, blk), blk), :]

    prev = jnp.maximum(i - 1, 0)
    carry, acc = tile([key_block(k_ref, prev), key_block(k_ref, i)], [key_block(v_ref, prev), key_block(v_ref, i)],
                      jnp.zeros((blk, 1), F32), jnp.zeros((blk, HEAD_DIM), F32), True, i > 0)
    n_left = prev

    def cond(c):
        return jnp.logical_and(c[0] < n_left // 2, c[1] <= SB_ZERO_MASS_LOG2)

    def body(c):
        j = prev - 2 * (c[0] + 1)
        carry, acc = tile([key_block(k_ref, j), key_block(k_ref, j + 1)],
                          [key_block(v_ref, j), key_block(v_ref, j + 1)], c[2], c[3], False, None)
        return c[0] + 1, jnp.min(carry), carry, acc

    _, min_mass, carry, acc = lax.while_loop(cond, body, (jnp.int32(0), jnp.min(carry), carry, acc))

    def last(c):
        return tile([key_block(k_ref, 0)], [key_block(v_ref, 0)], c[0], c[1], False, None)

    odd_left = jnp.logical_and(n_left % 2 == 1, min_mass <= SB_ZERO_MASS_LOG2)
    carry, acc = lax.cond(odd_left, last, lambda c: c, (carry, acc))
    o_ref[...] = acc.astype(o_ref.dtype)


def _sb_attention(qkv, batch, seq, *, blk):
    assert seq % (2 * blk) == 0
    nq = seq // blk
    h = SB_HEADS
    return pl.pallas_call(
        functools.partial(_attn_kernel, blk=blk, scale=HEAD_DIM ** -0.5),
        grid=(batch, h, nq),
        in_specs=[pl.BlockSpec((blk, HEAD_DIM), lambda b, hh, i: (b * nq + i, hh)),
                  pl.BlockSpec((seq, HEAD_DIM), lambda b, hh, i: (b, h + hh)),
                  pl.BlockSpec((seq, HEAD_DIM), lambda b, hh, i: (b, 2 * h + hh))],
        out_specs=pl.BlockSpec((blk, HEAD_DIM), lambda b, hh, i: (b * nq + i, hh)),
        out_shape=jax.ShapeDtypeStruct((batch * seq, SB_WIDTH), BF16),
        compiler_params=_cparams("parallel", "parallel", "arbitrary"),
        name="sb_attention",
    )(qkv, qkv, qkv)


def _halo_index(rows_per_tile):
    per = rows_per_tile // HALO
    return lambda t: jnp.maximum(t * per - 1, 0)


def _sconv_kernel(b_ref, c_ref, h_ref, ch_ref, hh_ref, w_ref, o_ref, xs_ref, *, tm):
    first = pl.program_id(1) == 0
    halo = ch_ref[...] * hh_ref[...]
    xs_ref[0:HALO, :] = jnp.where(first, 0.0, halo)
    xs_ref[HALO:HALO + tm, :] = c_ref[...] * h_ref[...]
    conv = jnp.zeros((tm, SC_WIDTH), F32)
    for i in range(SC_CONV):
        conv = conv + w_ref[i:i + 1, :] * xs_ref[pl.ds(HALO - SC_CONV + 1 + i, tm), :]
    o_ref[...] = (b_ref[...] * conv).astype(o_ref.dtype)


def _short_conv(proj, conv_w, layer, batch, seq, *, tm):
    tm = min(tm, seq)
    nt = seq // tm
    hal = _halo_index(tm)
    tile = lambda cb: pl.BlockSpec((tm, SC_WIDTH), lambda b, i: (b * nt + i, cb))
    halo = lambda cb: pl.BlockSpec((HALO, SC_WIDTH), lambda b, i: (hal(b * nt + i), cb))
    return pl.pallas_call(
        functools.partial(_sconv_kernel, tm=tm),
        grid=(batch, nt),
        in_specs=[tile(0), tile(1), tile(2), halo(1), halo(2),
                  pl.BlockSpec((None, SC_CONV, SC_WIDTH), lambda b, i: (layer, 0, 0))],
        out_specs=pl.BlockSpec((tm, SC_WIDTH), lambda b, i: (b * nt + i, 0)),
        out_shape=jax.ShapeDtypeStruct((batch * seq, SC_WIDTH), BF16),
        scratch_shapes=[pltpu.VMEM((HALO + tm, SC_WIDTH), F32)],
        compiler_params=_cparams("parallel", "arbitrary"),
        name="short_conv",
    )(proj, proj, proj, proj, proj, conv_w)


def _gdn_kernel(q_ref, k_ref, v_ref, z_ref, qh_ref, kh_ref, vh_ref, g_ref,
                cwq_ref, cwk_ref, cwv_ref, alog_ref, dtb_ref, nw_ref,
                o_ref, state_ref, xs_ref, *, chunks):
    c = GDN_CHUNK
    rows = chunks * c
    first = pl.program_id(1) == 0

    @pl.when(first)
    def _():
        state_ref[...] = jnp.zeros_like(state_ref)

    def conv_silu(slot, x_ref, halo_ref, w_ref):
        xs_ref[slot, 0:HALO, :] = jnp.where(first, 0.0, halo_ref[...])
        xs_ref[slot, HALO:HALO + rows, :] = x_ref[...]
        x = xs_ref[slot]
        y = w_ref[GDN_CONV - 1:GDN_CONV, :] * x[HALO:]
        for back in range(1, GDN_CONV):
            tap = w_ref[GDN_CONV - 1 - back:GDN_CONV - back, :]
            y = y + tap * pltpu.roll(x, back, axis=0)[HALO:]
        return y * jax.nn.sigmoid(y)

    q_all = conv_silu(0, q_ref, qh_ref, cwq_ref)
    k_all = conv_silu(1, k_ref, kh_ref, cwk_ref)
    v_all = conv_silu(2, v_ref, vh_ref, cwv_ref)

    gates = g_ref[...]
    g_all = -jnp.exp(alog_ref[...]) * _softplus(gates + dtb_ref[...])
    beta_all = jax.nn.sigmoid(gates)

    row = lax.broadcasted_iota(jnp.int32, (c, c), 0)
    col = lax.broadcasted_iota(jnp.int32, (c, c), 1)
    tri = col <= row
    strict = col < row
    lower_ones = jnp.where(tri, 1.0, 0.0)
    eye = jnp.where(row == col, 1.0, 0.0)
    lane_ones = jnp.ones((HEAD_DIM, HEAD_DIM), F32)

    heads = range(GDN_HEADS)
    pairs = [(ch, h) for ch in range(chunks) for h in heads]
    rs = lambda ch: slice(ch * c, (ch + 1) * c)
    hs = lambda h: slice(h * HEAD_DIM, (h + 1) * HEAD_DIM)

    gcum_col = [_dot(lower_ones, g_all[rs(ch)]) for ch in range(chunks)]
    gcum_row = [_dot_nt(g_all[rs(ch)].T, lower_ones) for ch in range(chunks)]

    q, k, v, beta, g_c, g_last, e_g, decay = {}, {}, {}, {}, {}, {}, {}, {}
    for p in pairs:
        ch, h = p
        qq = q_all[rs(ch), hs(h)]
        kk = k_all[rs(ch), hs(h)]
        q[p] = qq * lax.rsqrt(jnp.sum(qq * qq, axis=-1, keepdims=True) + L2_EPS) * (HEAD_DIM ** -0.5)
        k[p] = kk * lax.rsqrt(jnp.sum(kk * kk, axis=-1, keepdims=True) + L2_EPS)
        v[p] = v_all[rs(ch), hs(h)]
        beta[p] = beta_all[rs(ch), GDN_HEADS + h:GDN_HEADS + h + 1]
        g_c[p] = gcum_col[ch][:, h:h + 1]
        g_r = gcum_row[ch][h:h + 1, :]
        g_last[p] = g_c[p][c - 1:c, :]
        e_g[p] = jnp.exp(g_c[p])
        decay[p] = jnp.where(tri, jnp.exp(jnp.where(tri, g_c[p] - g_r, 0.0)), 0.0)

    k16 = {p: k[p].astype(BF16) for p in pairs}
    kk = {p: _dot_nt(k16[p], k16[p]) for p in pairs}
    qk = {p: _dot_nt(q[p].astype(BF16), k16[p]) for p in pairs}
    attn = {p: jnp.where(tri, qk[p] * decay[p], 0.0).astype(BF16) for p in pairs}
    pw = {p: jnp.where(strict, -(beta[p] * kk[p] * decay[p]), 0.0) for p in pairs}
    t_mat = {p: eye + pw[p] for p in pairs}
    for _ in range(5):
        pw = {p: _dot(pw[p], pw[p]) for p in pairs}
        t_mat = {p: t_mat[p] + _dot(t_mat[p], pw[p]) for p in pairs}
    rhs = {p: jnp.concatenate([v[p] * beta[p], k[p] * (beta[p] * e_g[p])], axis=1).astype(BF16) for p in pairs}
    uw = {p: _dot(t_mat[p].astype(BF16), rhs[p]) for p in pairs}
    wq = {p: jnp.concatenate([uw[p][:, HEAD_DIM:], q[p] * e_g[p]], axis=0).astype(BF16) for p in pairs}
    k_dec = {p: (k[p] * jnp.exp(g_last[p] - g_c[p])).astype(BF16) for p in pairs}

    state = {h: state_ref[h] for h in heads}
    out = {}
    for ch in range(chunks):
        s16 = {h: state[h].astype(BF16) for h in heads}
        ws = {h: _dot(wq[ch, h], s16[h]) for h in heads}
        v_new = {h: (uw[ch, h][:, :HEAD_DIM] - ws[h][:c]).astype(BF16) for h in heads}
        out.update({(ch, h): ws[h][c:] + _dot(attn[ch, h], v_new[h]) for h in heads})
        state = {h: state[h] * jnp.exp(g_last[ch, h]) + _dot_tn(k_dec[ch, h], v_new[h]) for h in heads}
    for h in heads:
        state_ref[h] = state[h]

    for p in pairs:
        ch, h = p
        o = out[p]
        r = lax.rsqrt(jnp.mean(o * o, axis=-1, keepdims=True) + RMS_EPS)
        zz = z_ref[rs(ch), hs(h)]
        y = (o * r * nw_ref[...]) * (zz * jax.nn.sigmoid(zz))
        o_ref[rs(ch), hs(h)] = y.astype(o_ref.dtype)


def _gdn_mixer(proj, gates, conv_w, a_log, dt_bias, norm_w, layer, batch, seq, col0, *, chunks_per_step):
    rows = chunks_per_step * GDN_CHUNK
    assert seq % rows == 0
    nt = seq // rows
    cb0 = col0 // GDN_WIDTH
    hal = _halo_index(rows)
    tile = lambda cb: pl.BlockSpec((rows, GDN_WIDTH), lambda b, i: (b * nt + i, cb0 + cb))
    halo = lambda cb: pl.BlockSpec((HALO, GDN_WIDTH), lambda b, i: (hal(b * nt + i), cb0 + cb))
    cw = lambda cb: pl.BlockSpec((None, GDN_CONV, GDN_WIDTH), lambda b, i: (layer, 0, cb))
    rowvec = pl.BlockSpec((None, 1, HEAD_DIM), lambda b, i: (layer, 0, 0))
    return pl.pallas_call(
        functools.partial(_gdn_kernel, chunks=chunks_per_step),
        grid=(batch, nt),
        in_specs=[tile(0), tile(1), tile(2), tile(3), halo(0), halo(1), halo(2),
                  pl.BlockSpec((rows, HEAD_DIM), lambda b, i: (b * nt + i, 0)),
                  cw(0), cw(1), cw(2), rowvec, rowvec, rowvec],
        out_specs=pl.BlockSpec((rows, GDN_WIDTH), lambda b, i: (b * nt + i, 0)),
        out_shape=jax.ShapeDtypeStruct((batch * seq, GDN_WIDTH), BF16),
        scratch_shapes=[pltpu.VMEM((GDN_HEADS, HEAD_DIM, HEAD_DIM), F32),
                        pltpu.VMEM((3, HALO + rows, GDN_WIDTH), F32)],
        compiler_params=_cparams("parallel", "arbitrary"),
        name="gdn_mixer",
    )(proj, proj, proj, proj, proj, proj, proj, gates,
      conv_w, conv_w, conv_w, a_log, dt_bias, norm_w)


def _out_proj_kernel(ysb_ref, ysc_ref, ygdn_ref, w_ref, res_ref, o_ref):
    acc = res_ref[...]
    acc = acc + _dot(ysb_ref[...], w_ref[0:SB_WIDTH, :])
    acc = acc + _dot(ysc_ref[...], w_ref[SB_WIDTH:SB_WIDTH + SC_WIDTH, :])
    acc = acc + _dot(ygdn_ref[...], w_ref[SB_WIDTH + SC_WIDTH:, :])
    o_ref[...] = acc


def _out_proj(y_sb, y_sc, y_gdn, w, layer, res, *, tm, tn):
    m, d = res.shape
    kdim = w.shape[1]
    tm, tn = min(tm, m), min(tn, d)
    rows = lambda width: pl.BlockSpec((tm, width), lambda i, j: (i, 0))
    return pl.pallas_call(
        _out_proj_kernel,
        grid=(m // tm, d // tn),
        in_specs=[rows(SB_WIDTH), rows(SC_WIDTH), rows(GDN_WIDTH),
                  pl.BlockSpec((None, kdim, tn), lambda i, j: (layer, 0, j)),
                  pl.BlockSpec((tm, tn), lambda i, j: (i, j))],
        out_specs=pl.BlockSpec((tm, tn), lambda i, j: (i, j)),
        out_shape=jax.ShapeDtypeStruct((m, d), F32),
        compiler_params=_cparams("parallel", "parallel"),
        name="out_proj",
    )(y_sb, y_sc, y_gdn, w, res)


def _matmul_res_kernel(a_ref, w_ref, res_ref, nw_ref, o_ref, *, final_norm):
    k = pl.program_id(1)

    @pl.when(k == 0)
    def _():
        o_ref[...] = res_ref[...]

    o_ref[...] += _dot(a_ref[...], w_ref[...])

    if final_norm:
        @pl.when(k == pl.num_programs(1) - 1)
        def _():
            x = o_ref[...]
            r = lax.rsqrt(jnp.mean(x * x, axis=-1, keepdims=True) + RMS_EPS)
            o_ref[...] = x * r * nw_ref[...]


def _matmul_res(a, w, layer, res, norm_w, *, tm, tk, final_norm):
    m, kdim = a.shape
    n = w.shape[2]
    tm, tk = min(tm, m), min(tk, kdim)
    assert m % tm == 0 and kdim % tk == 0
    return pl.pallas_call(
        functools.partial(_matmul_res_kernel, final_norm=final_norm),
        grid=(m // tm, kdim // tk),
        in_specs=[pl.BlockSpec((tm, tk), lambda i, k: (i, k)),
                  pl.BlockSpec((None, tk, n), lambda i, k: (layer, k, 0)),
                  pl.BlockSpec((tm, n), lambda i, k: (i, 0)),
                  pl.BlockSpec((1, n), lambda i, k: (0, 0))],
        out_specs=pl.BlockSpec((tm, n), lambda i, k: (i, 0)),
        out_shape=jax.ShapeDtypeStruct((m, n), F32),
        compiler_params=_cparams("parallel", "arbitrary"),
        name="matmul_res",
    )(a, w, res, norm_w.reshape(1, n))


def kernel(x, norm1_w, w_in, sc_conv_w, gdn_conv_w, gdn_a_log, gdn_dt_bias, gdn_norm_w, w_out, norm2_w,
           w_up, w_down, final_norm_w):
    batch, seq, d = x.shape
    layers = norm1_w.shape[0]
    w_in16 = _cast_pad_w_in(w_in, rows=256)
    w_out16, w_up16, w_down16 = (w.astype(BF16) for w in (w_out, w_up, w_down))
    pad_heads = lambda a: jnp.zeros((layers, 1, HEAD_DIM), F32).at[:, 0, :GDN_HEADS].set(a)
    a_log, dt_bias = pad_heads(gdn_a_log), pad_heads(gdn_dt_bias)
    gdn_nw = gdn_norm_w.reshape(layers, 1, HEAD_DIM)

    h = x.reshape(batch * seq, d)
    for l in range(layers):
        qkv_sb, proj, gates = _in_proj(h, norm1_w, w_in16, l, tm=1024)
        y_sb = _sb_attention(qkv_sb, batch, seq, blk=256)
        y_sc = _short_conv(proj, sc_conv_w, l, batch, seq, tm=512)
        y_gdn = _gdn_mixer(proj, gates, gdn_conv_w, a_log, dt_bias, gdn_nw, l, batch, seq, 3 * SC_WIDTH,
                           chunks_per_step=4)
        h = _out_proj(y_sb, y_sc, y_gdn, w_out16, l, h, tm=1024, tn=1024)
        hidden = _norm_matmul(h, norm2_w, w_up16, l, BF16, tm=1024, tn=1024, sq_relu=True)
        h = _matmul_res(hidden, w_down16, l, h, final_norm_w, tm=512, tk=2048, final_norm=l == layers - 1)
    return h.reshape(batch, seq, d)
```

```python
import functools

import jax
import jax.numpy as jnp
from jax import lax
from jax.experimental import pallas as pl
from jax.experimental.pallas import tpu as pltpu

HEAD_DIM = 128
SB_HEADS = 6
SB_WIDTH = SB_HEADS * HEAD_DIM
SC_WIDTH = 4 * HEAD_DIM
GDN_HEADS = 6
GDN_WIDTH = GDN_HEADS * HEAD_DIM
SC_CONV = 3
GDN_CONV = 4
GDN_CHUNK = 64
RMS_EPS = 1e-6
L2_EPS = 1e-6
HALO = 8
VMEM_LIMIT_BYTES = 56 * 1024 * 1024

F32 = jnp.float32
BF16 = jnp.bfloat16


def _cparams(*sem):
    return pltpu.CompilerParams(dimension_semantics=sem, vmem_limit_bytes=VMEM_LIMIT_BYTES)


def _dot(a, b):
    return jnp.dot(a, b, preferred_element_type=F32)


def _dot_nt(a, b):
    return lax.dot_general(a, b, (((1,), (1,)), ((), ())), preferred_element_type=F32)


def _dot_tn(a, b):
    return lax.dot_general(a, b, (((0,), (0,)), ((), ())), preferred_element_type=F32)


def _rms_normalize(x_ref, nw_ref):
    x = x_ref[...]
    r = lax.rsqrt(jnp.mean(x * x, axis=-1, keepdims=True) + RMS_EPS)
    return (x * r * nw_ref[...]).astype(BF16)


def _norm_matmul_kernel(x_ref, nw_ref, w_ref, o_ref, hn_ref, *, sq_relu):
    @pl.when(pl.program_id(1) == 0)
    def _():
        hn_ref[...] = _rms_normalize(x_ref, nw_ref)

    acc = _dot(hn_ref[...], w_ref[...])
    if sq_relu:
        acc = jnp.square(jnp.maximum(acc, 0.0))
    o_ref[...] = acc.astype(o_ref.dtype)


def _norm_matmul(x, norm_w, w, layer, out_dtype, *, tm, tn, sq_relu=False):
    m, d = x.shape
    n = w.shape[2]
    tm, tn = min(tm, m), min(tn, n)
    assert n % tn == 0 and m % tm == 0
    return pl.pallas_call(
        functools.partial(_norm_matmul_kernel, sq_relu=sq_relu),
        grid=(m // tm, n // tn),
        in_specs=[pl.BlockSpec((tm, d), lambda i, j: (i, 0)),
                  pl.BlockSpec((None, 1, d), lambda i, j: (layer, 0, 0)),
                  pl.BlockSpec((None, d, tn), lambda i, j: (layer, 0, j))],
        out_specs=pl.BlockSpec((tm, tn), lambda i, j: (i, j)),
        out_shape=jax.ShapeDtypeStruct((m, n), out_dtype),
        scratch_shapes=[pltpu.VMEM((tm, d), BF16)],
        compiler_params=_cparams("parallel", "arbitrary"),
        name="norm_matmul",
    )(x, norm_w.reshape(norm_w.shape[0], 1, d), w)


IN_TILE = 768
SB_COLS = 3 * SB_WIDTH
PROJ_COLS = 3 * SC_WIDTH + 4 * GDN_WIDTH
IN_DIM = SB_COLS + PROJ_COLS + 2 * GDN_HEADS
SB_TILES = SB_COLS // IN_TILE
PROJ_TILES = PROJ_COLS // IN_TILE


def _in_proj_kernel(x_ref, nw_ref, w_ref, sb_ref, proj_ref, gate_ref, hn_ref):
    j = pl.program_id(1)

    @pl.when(j == 0)
    def _():
        hn_ref[...] = _rms_normalize(x_ref, nw_ref)

    @pl.when(j < SB_TILES)
    def _():
        sb_ref[...] = _dot(hn_ref[...], w_ref[...]).astype(sb_ref.dtype)

    @pl.when(jnp.logical_and(j >= SB_TILES, j < SB_TILES + PROJ_TILES))
    def _():
        proj_ref[...] = _dot(hn_ref[...], w_ref[...])

    @pl.when(j == SB_TILES + PROJ_TILES)
    def _():
        w = w_ref[:, :HEAD_DIM]
        lane = lax.broadcasted_iota(jnp.int32, w.shape, 1)
        gate_ref[...] = _dot(hn_ref[...], jnp.where(lane < 2 * GDN_HEADS, w, jnp.zeros_like(w)))


def _in_proj(x, norm_w, w_in16, layer, *, tm):
    m, d = x.shape
    tm = min(tm, m)
    assert m % tm == 0 and w_in16.shape[2] == IN_DIM
    return pl.pallas_call(
        _in_proj_kernel,
        grid=(m // tm, SB_TILES + PROJ_TILES + 1),
        in_specs=[pl.BlockSpec((tm, d), lambda i, j: (i, 0)),
                  pl.BlockSpec((None, 1, d), lambda i, j: (layer, 0, 0)),
                  pl.BlockSpec((None, d, IN_TILE), lambda i, j: (layer, 0, j))],
        out_specs=[pl.BlockSpec((tm, IN_TILE), lambda i, j: (i, jnp.minimum(j, SB_TILES - 1))),
                   pl.BlockSpec((tm, IN_TILE), lambda i, j: (i, jnp.clip(j - SB_TILES, 0, PROJ_TILES - 1))),
                   pl.BlockSpec((tm, HEAD_DIM), lambda i, j: (i, 0))],
        out_shape=[jax.ShapeDtypeStruct((m, SB_COLS), BF16),
                   jax.ShapeDtypeStruct((m, PROJ_COLS), F32),
                   jax.ShapeDtypeStruct((m, HEAD_DIM), F32)],
        scratch_shapes=[pltpu.VMEM((tm, d), BF16)],
        compiler_params=_cparams("parallel", "arbitrary"),
        name="in_proj",
    )(x, norm_w.reshape(norm_w.shape[0], 1, d), w_in16)


def _softplus(z):
    return jnp.maximum(z, 0.0) + jnp.log(1.0 + jnp.exp(-jnp.abs(z)))


LOG2_E = 1.4426950408889634
SB_ZERO_MASS_LOG2 = 160.0
SB_ROW_GROUPS = 2


def _softplus_log2(z2):
    return jnp.maximum(z2, 0.0) + jnp.log(1.0 + jnp.exp2(-jnp.abs(z2))) * LOG2_E


def _attn_kernel(q_ref, k_ref, v_ref, o_ref, *, blk, scale, heads):
    i = pl.program_id(2)
    row = lax.broadcasted_iota(jnp.int32, (blk, blk), 0)
    col = lax.broadcasted_iota(jnp.int32, (blk, blk), 1)
    suffix_ones = jnp.where(row >= col, 1.0, 0.0).astype(BF16)
    causal = col < row
    lanes = lambda hd: slice(hd * HEAD_DIM, (hd + 1) * HEAD_DIM)
    step = blk // SB_ROW_GROUPS
    groups = [(hd, slice(r * step, (r + 1) * step)) for hd in range(heads) for r in range(SB_ROW_GROUPS)]
    q = [q_ref[:, lanes(hd)] for hd in range(heads)]

    def suffix_sum(sp):
        hi = sp.astype(BF16)
        lo = (sp - hi.astype(F32)).astype(BF16)
        return _dot(hi, suffix_ones) + _dot(lo, suffix_ones)

    def key_block(ref, j, hd):
        return ref[pl.ds(pl.multiple_of(j * blk, blk), blk), lanes(hd)]

    def tile(blocks, carry, acc, diagonal_last, keep_first):
        n = len(blocks)
        kb = [jnp.concatenate([key_block(k_ref, j, hd) for j in blocks], axis=0) if n > 1
              else key_block(k_ref, blocks[0], hd) for hd in range(heads)]
        vb = [[key_block(v_ref, j, hd) for j in blocks] for hd in range(heads)]
        z2 = [_dot_nt(q[hd][g], kb[hd]) * (scale * LOG2_E) for hd, g in groups]
        sp = [_softplus_log2(z) for z in z2]
        carry = [carry[hd][g] for hd, g in groups]
        acc = [acc[hd][g] for hd, g in groups]
        for part in reversed(range(n)):
            cols = slice(part * blk, (part + 1) * blk)
            on_diagonal = diagonal_last and part == n - 1
            mass, w = [], []
            for x, (hd, g) in enumerate(groups):
                sp_part = jnp.where(causal[g], sp[x][:, cols], 0.0) if on_diagonal else sp[x][:, cols]
                mass.append(suffix_sum(sp_part) + carry[x])
                carry[x] = mass[x][:, 0:1]
            for x, (hd, g) in enumerate(groups):
                wx = jnp.exp2(z2[x][:, cols] - mass[x])
                w.append((jnp.where(causal[g], wx, 0.0) if on_diagonal else wx).astype(BF16))
            for x, (hd, g) in enumerate(groups):
                contrib = _dot(w[x], vb[hd][part])
                if part == 0 and keep_first is not None:
                    contrib = jnp.where(keep_first, contrib, 0.0)
                acc[x] = acc[x] + contrib
        per_head = lambda vals: [jnp.concatenate([v for v, (h2, _) in zip(vals, groups) if h2 == hd], axis=0)
                                 for hd in range(heads)]
        return per_head(carry), per_head(acc)

    def min_mass_of(carry):
        return functools.reduce(jnp.minimum, [jnp.min(c) for c in carry])

    prev = jnp.maximum(i - 1, 0)
    carry, acc = tile([prev, i], [jnp.zeros((blk, 1), F32)] * heads, [jnp.zeros((blk, HEAD_DIM), F32)] * heads,
                      True, i > 0)
    n_left = prev

    def cond(c):
        return jnp.logical_and(c[0] < n_left // 2, c[1] <= SB_ZERO_MASS_LOG2)

    def body(c):
        j = prev - 2 * (c[0] + 1)
        carry, acc = tile([j, j + 1], c[2], c[3], False, None)
        return c[0] + 1, min_mass_of(carry), carry, acc

    _, min_mass, carry, acc = lax.while_loop(cond, body, (jnp.int32(0), min_mass_of(carry), carry, acc))

    def last(c):
        return tile([0], c[0], c[1], False, None)

    odd_left = jnp.logical_and(n_left % 2 == 1, min_mass <= SB_ZERO_MASS_LOG2)
    carry, acc = lax.cond(odd_left, last, lambda c: c, (carry, acc))
    for hd in range(heads):
        o_ref[:, lanes(hd)] = acc[hd].astype(o_ref.dtype)


def _sb_attention(qkv, batch, seq, *, blk, heads_per_step):
    assert seq % (2 * blk) == 0 and SB_HEADS % heads_per_step == 0
    nq = seq // blk
    width = heads_per_step * HEAD_DIM
    hsteps = SB_HEADS // heads_per_step
    return pl.pallas_call(
        functools.partial(_attn_kernel, blk=blk, scale=HEAD_DIM ** -0.5, heads=heads_per_step),
        grid=(batch, hsteps, nq),
        in_specs=[pl.BlockSpec((blk, width), lambda b, hh, i: (b * nq + i, hh)),
                  pl.BlockSpec((seq, width), lambda b, hh, i: (b, hsteps + hh)),
                  pl.BlockSpec((seq, width), lambda b, hh, i: (b, 2 * hsteps + hh))],
        out_specs=pl.BlockSpec((blk, width), lambda b, hh, i: (b * nq + i, hh)),
        out_shape=jax.ShapeDtypeStruct((batch * seq, SB_WIDTH), BF16),
        compiler_params=_cparams("parallel", "parallel", "arbitrary"),
        name="sb_attention",
    )(qkv, qkv, qkv)


def _halo_index(rows_per_tile):
    per = rows_per_tile // HALO
    return lambda t: jnp.maximum(t * per - 1, 0)


def _sconv_kernel(b_ref, c_ref, h_ref, ch_ref, hh_ref, w_ref, o_ref, xs_ref, *, tm):
    first = pl.program_id(1) == 0
    halo = ch_ref[...] * hh_ref[...]
    xs_ref[0:HALO, :] = jnp.where(first, 0.0, halo)
    xs_ref[HALO:HALO + tm, :] = c_ref[...] * h_ref[...]
    conv = jnp.zeros((tm, SC_WIDTH), F32)
    for i in range(SC_CONV):
        conv = conv + w_ref[i:i + 1, :] * xs_ref[pl.ds(HALO - SC_CONV + 1 + i, tm), :]
    o_ref[...] = (b_ref[...] * conv).astype(o_ref.dtype)


def _short_conv(proj, conv_w, layer, batch, seq, *, tm):
    tm = min(tm, seq)
    nt = seq // tm
    hal = _halo_index(tm)
    tile = lambda cb: pl.BlockSpec((tm, SC_WIDTH), lambda b, i: (b * nt + i, cb))
    halo = lambda cb: pl.BlockSpec((HALO, SC_WIDTH), lambda b, i: (hal(b * nt + i), cb))
    return pl.pallas_call(
        functools.partial(_sconv_kernel, tm=tm),
        grid=(batch, nt),
        in_specs=[tile(0), tile(1), tile(2), halo(1), halo(2),
                  pl.BlockSpec((None, SC_CONV, SC_WIDTH), lambda b, i: (layer, 0, 0))],
        out_specs=pl.BlockSpec((tm, SC_WIDTH), lambda b, i: (b * nt + i, 0)),
        out_shape=jax.ShapeDtypeStruct((batch * seq, SC_WIDTH), BF16),
        scratch_shapes=[pltpu.VMEM((HALO + tm, SC_WIDTH), F32)],
        compiler_params=_cparams("parallel", "arbitrary"),
        name="short_conv",
    )(proj, proj, proj, proj, proj, conv_w)


def _gdn_kernel(q_ref, k_ref, v_ref, z_ref, qh_ref, kh_ref, vh_ref, g_ref,
                cwq_ref, cwk_ref, cwv_ref, alog_ref, dtb_ref, nw_ref,
                o_ref, state_ref, xs_ref, *, chunks):
    c = GDN_CHUNK
    rows = chunks * c
    first = pl.program_id(1) == 0

    @pl.when(first)
    def _():
        state_ref[...] = jnp.zeros_like(state_ref)

    def conv_silu(slot, x_ref, halo_ref, w_ref):
        xs_ref[slot, 0:HALO, :] = jnp.where(first, 0.0, halo_ref[...])
        xs_ref[slot, HALO:HALO + rows, :] = x_ref[...]
        x = xs_ref[slot]
        y = w_ref[GDN_CONV - 1:GDN_CONV, :] * x[HALO:]
        for back in range(1, GDN_CONV):
            tap = w_ref[GDN_CONV - 1 - back:GDN_CONV - back, :]
            y = y + tap * pltpu.roll(x, back, axis=0)[HALO:]
        return y * jax.nn.sigmoid(y)

    q_all = conv_silu(0, q_ref, qh_ref, cwq_ref)
    k_all = conv_silu(1, k_ref, kh_ref, cwk_ref)
    v_all = conv_silu(2, v_ref, vh_ref, cwv_ref)

    gates = g_ref[...]
    g_all = -jnp.exp(alog_ref[...]) * _softplus(gates + dtb_ref[...])
    beta_all = jax.nn.sigmoid(gates)

    row = lax.broadcasted_iota(jnp.int32, (c, c), 0)
    col = lax.broadcasted_iota(jnp.int32, (c, c), 1)
    tri = col <= row
    strict = col < row
    lower_ones = jnp.where(tri, 1.0, 0.0)
    eye = jnp.where(row == col, 1.0, 0.0)
    lane_ones = jnp.ones((HEAD_DIM, HEAD_DIM), F32)

    heads = range(GDN_HEADS)
    pairs = [(ch, h) for ch in range(chunks) for h in heads]
    rs = lambda ch: slice(ch * c, (ch + 1) * c)
    hs = lambda h: slice(h * HEAD_DIM, (h + 1) * HEAD_DIM)

    gcum_col = [_dot(lower_ones, g_all[rs(ch)]) for ch in range(chunks)]
    gcum_row = [_dot_nt(g_all[rs(ch)].T, lower_ones) for ch in range(chunks)]

    q, k, v, beta, g_c, g_last, e_g, decay = {}, {}, {}, {}, {}, {}, {}, {}
    for p in pairs:
        ch, h = p
        qq = q_all[rs(ch), hs(h)]
        kk = k_all[rs(ch), hs(h)]
        q[p] = qq * lax.rsqrt(jnp.sum(qq * qq, axis=-1, keepdims=True) + L2_EPS) * (HEAD_DIM ** -0.5)
        k[p] = kk * lax.rsqrt(jnp.sum(kk * kk, axis=-1, keepdims=True) + L2_EPS)
        v[p] = v_all[rs(ch), hs(h)]
        beta[p] = beta_all[rs(ch), GDN_HEADS + h:GDN_HEADS + h + 1]
        g_c[p] = gcum_col[ch][:, h:h + 1]
        g_r = gcum_row[ch][h:h + 1, :]
        g_last[p] = g_c[p][c - 1:c, :]
        e_g[p] = jnp.exp(g_c[p])
        decay[p] = jnp.where(tri, jnp.exp(jnp.where(tri, g_c[p] - g_r, 0.0)), 0.0)

    k16 = {p: k[p].astype(BF16) for p in pairs}
    kk = {p: _dot_nt(k16[p], k16[p]) for p in pairs}
    qk = {p: _dot_nt(q[p].astype(BF16), k16[p]) for p in pairs}
    attn = {p: jnp.where(tri, qk[p] * decay[p], 0.0).astype(BF16) for p in pairs}
    pw = {p: jnp.where(strict, -(beta[p] * kk[p] * decay[p]), 0.0) for p in pairs}
    t_mat = {p: eye + pw[p] for p in pairs}
    for _ in range(5):
        pw = {p: _dot(pw[p], pw[p]) for p in pairs}
        t_mat = {p: t_mat[p] + _dot(t_mat[p], pw[p]) for p in pairs}
    rhs = {p: jnp.concatenate([v[p] * beta[p], k[p] * (beta[p] * e_g[p])], axis=1).astype(BF16) for p in pairs}
    uw = {p: _dot(t_mat[p].astype(BF16), rhs[p]) for p in pairs}
    wq = {p: jnp.concatenate([uw[p][:, HEAD_DIM:], q[p] * e_g[p]], axis=0).astype(BF16) for p in pairs}
    k_dec = {p: (k[p] * jnp.exp(g_last[p] - g_c[p])).astype(BF16) for p in pairs}

    state = {h: state_ref[h] for h in heads}
    out = {}
    for ch in range(chunks):
        s16 = {h: state[h].astype(BF16) for h in heads}
        ws = {h: _dot(wq[ch, h], s16[h]) for h in heads}
        v_new = {h: (uw[ch, h][:, :HEAD_DIM] - ws[h][:c]).astype(BF16) for h in heads}
        out.update({(ch, h): ws[h][c:] + _dot(attn[ch, h], v_new[h]) for h in heads})
        state = {h: state[h] * jnp.exp(g_last[ch, h]) + _dot_tn(k_dec[ch, h], v_new[h]) for h in heads}
    for h in heads:
        state_ref[h] = state[h]

    for p in pairs:
        ch, h = p
        o = out[p]
        r = lax.rsqrt(jnp.mean(o * o, axis=-1, keepdims=True) + RMS_EPS)
        zz = z_ref[rs(ch), hs(h)]
        y = (o * r * nw_ref[...]) * (zz * jax.nn.sigmoid(zz))
        o_ref[rs(ch), hs(h)] = y.astype(o_ref.dtype)


def _gdn_mixer(proj, gates, conv_w, a_log, dt_bias, norm_w, layer, batch, seq, col0, *, chunks_per_step):
    rows = chunks_per_step * GDN_CHUNK
    assert seq % rows == 0
    nt = seq // rows
    cb0 = col0 // GDN_WIDTH
    hal = _halo_index(rows)
    tile = lambda cb: pl.BlockSpec((rows, GDN_WIDTH), lambda b, i: (b * nt + i, cb0 + cb))
    halo = lambda cb: pl.BlockSpec((HALO, GDN_WIDTH), lambda b, i: (hal(b * nt + i), cb0 + cb))
    cw = lambda cb: pl.BlockSpec((None, GDN_CONV, GDN_WIDTH), lambda b, i: (layer, 0, cb))
    rowvec = pl.BlockSpec((None, 1, HEAD_DIM), lambda b, i: (layer, 0, 0))
    return pl.pallas_call(
        functools.partial(_gdn_kernel, chunks=chunks_per_step),
        grid=(batch, nt),
        in_specs=[tile(0), tile(1), tile(2), tile(3), halo(0), halo(1), halo(2),
                  pl.BlockSpec((rows, HEAD_DIM), lambda b, i: (b * nt + i, 0)),
                  cw(0), cw(1), cw(2), rowvec, rowvec, rowvec],
        out_specs=pl.BlockSpec((rows, GDN_WIDTH), lambda b, i: (b * nt + i, 0)),
        out_shape=jax.ShapeDtypeStruct((batch * seq, GDN_WIDTH), BF16),
        scratch_shapes=[pltpu.VMEM((GDN_HEADS, HEAD_DIM, HEAD_DIM), F32),
                        pltpu.VMEM((3, HALO + rows, GDN_WIDTH), F32)],
        compiler_params=_cparams("parallel", "arbitrary"),
        name="gdn_mixer",
    )(proj, proj, proj, proj, proj, proj, proj, gates,
      conv_w, conv_w, conv_w, a_log, dt_bias, norm_w)


def _out_proj_kernel(ysb_ref, ysc_ref, ygdn_ref, w_ref, res_ref, o_ref):
    acc = res_ref[...]
    acc = acc + _dot(ysb_ref[...], w_ref[0:SB_WIDTH, :])
    acc = acc + _dot(ysc_ref[...], w_ref[SB_WIDTH:SB_WIDTH + SC_WIDTH, :])
    acc = acc + _dot(ygdn_ref[...], w_ref[SB_WIDTH + SC_WIDTH:, :])
    o_ref[...] = acc


def _out_proj(y_sb, y_sc, y_gdn, w, layer, res, *, tm, tn):
    m, d = res.shape
    kdim = w.shape[1]
    tm, tn = min(tm, m), min(tn, d)
    rows = lambda width: pl.BlockSpec((tm, width), lambda i, j: (i, 0))
    return pl.pallas_call(
        _out_proj_kernel,
        grid=(m // tm, d // tn),
        in_specs=[rows(SB_WIDTH), rows(SC_WIDTH), rows(GDN_WIDTH),
                  pl.BlockSpec((None, kdim, tn), lambda i, j: (layer, 0, j)),
                  pl.BlockSpec((tm, tn), lambda i, j: (i, j))],
        out_specs=pl.BlockSpec((tm, tn), lambda i, j: (i, j)),
        out_shape=jax.ShapeDtypeStruct((m, d), F32),
        compiler_params=_cparams("parallel", "parallel"),
        name="out_proj",
    )(y_sb, y_sc, y_gdn, w, res)


def _matmul_res_kernel(a_ref, w_ref, res_ref, nw_ref, o_ref, *, final_norm):
    k = pl.program_id(2)

    @pl.when(k == 0)
    def _():
        o_ref[...] = res_ref[...]

    o_ref[...] += _dot(a_ref[...], w_ref[...])

    if final_norm:
        @pl.when(k == pl.num_programs(2) - 1)
        def _():
            x = o_ref[...]
            r = lax.rsqrt(jnp.mean(x * x, axis=-1, keepdims=True) + RMS_EPS)
            o_ref[...] = x * r * nw_ref[...]


def _matmul_res(a, w, layer, res, norm_w, *, tm, tn, tk, final_norm):
    m, kdim = a.shape
    n = w.shape[2]
    tm, tn, tk = min(tm, m), min(tn, n), min(tk, kdim)
    assert m % tm == 0 and n % tn == 0 and kdim % tk == 0 and (tn == n or not final_norm)
    return pl.pallas_call(
        functools.partial(_matmul_res_kernel, final_norm=final_norm),
        grid=(m // tm, n // tn, kdim // tk),
        in_specs=[pl.BlockSpec((tm, tk), lambda i, j, k: (i, k)),
                  pl.BlockSpec((None, tk, tn), lambda i, j, k: (layer, k, j)),
                  pl.BlockSpec((tm, tn), lambda i, j, k: (i, j)),
                  pl.BlockSpec((1, tn), lambda i, j, k: (0, j))],
        out_specs=pl.BlockSpec((tm, tn), lambda i, j, k: (i, j)),
        out_shape=jax.ShapeDtypeStruct((m, n), F32),
        compiler_params=_cparams("parallel", "parallel", "arbitrary"),
        name="matmul_res",
    )(a, w, res, norm_w.reshape(1, n))


def kernel(x, norm1_w, w_in, sc_conv_w, gdn_conv_w, gdn_a_log, gdn_dt_bias, gdn_norm_w, w_out, norm2_w,
           w_up, w_down, final_norm_w):
    batch, seq, d = x.shape
    layers = norm1_w.shape[0]
    w_in16, w_out16, w_up16, w_down16 = (w.astype(BF16) for w in (w_in, w_out, w_up, w_down))
    pad_heads = lambda a: jnp.zeros((layers, 1, HEAD_DIM), F32).at[:, 0, :GDN_HEADS].set(a)
    a_log, dt_bias = pad_heads(gdn_a_log), pad_heads(gdn_dt_bias)
    gdn_nw = gdn_norm_w.reshape(layers, 1, HEAD_DIM)

    h = x.reshape(batch * seq, d)
    for l in range(layers):
        qkv_sb, proj, gates = _in_proj(h, norm1_w, w_in16, l, tm=1024)
        y_sb = _sb_attention(qkv_sb, batch, seq, blk=256, heads_per_step=6)
        y_sc = _short_conv(proj, sc_conv_w, l, batch, seq, tm=512)
        y_gdn = _gdn_mixer(proj, gates, gdn_conv_w, a_log, dt_bias, gdn_nw, l, batch, seq, 3 * SC_WIDTH,
                           chunks_per_step=4)
        h = _out_proj(y_sb, y_sc, y_gdn, w_out16, l, h, tm=1024, tn=1024)
        hidden = _norm_matmul(h, norm2_w, w_up16, l, BF16, tm=1024, tn=1024, sq_relu=True)
        last = l == layers - 1
        h = _matmul_res(hidden, w_down16, l, h, final_norm_w, tm=512 if last else 1024, tn=d if last else 1024,
                        tk=2048, final_norm=last)
    return h.reshape(batch, seq, d)
```

```python
import functools

import jax
import jax.numpy as jnp
from jax import lax
from jax.experimental import pallas as pl
from jax.experimental.pallas import tpu as pltpu

HEAD_DIM = 128
SB_HEADS = 6
SB_WIDTH = SB_HEADS * HEAD_DIM
SC_WIDTH = 4 * HEAD_DIM
GDN_HEADS = 6
GDN_WIDTH = GDN_HEADS * HEAD_DIM
SC_CONV = 3
GDN_CONV = 4
GDN_CHUNK = 64
RMS_EPS = 1e-6
L2_EPS = 1e-6
HALO = 8
VMEM_LIMIT_BYTES = 56 * 1024 * 1024

F32 = jnp.float32
BF16 = jnp.bfloat16


def _cparams(*sem):
    return pltpu.CompilerParams(dimension_semantics=sem, vmem_limit_bytes=VMEM_LIMIT_BYTES)


def _dot(a, b):
    return jnp.dot(a, b, preferred_element_type=F32)


def _dot_nt(a, b):
    return lax.dot_general(a, b, (((1,), (1,)), ((), ())), preferred_element_type=F32)


def _dot_tn(a, b):
    return lax.dot_general(a, b, (((0,), (0,)), ((), ())), preferred_element_type=F32)


def _rms_normalize(x_ref, nw_ref):
    x = x_ref[...]
    r = lax.rsqrt(jnp.mean(x * x, axis=-1, keepdims=True) + RMS_EPS)
    return (x * r * nw_ref[...]).astype(BF16)


def _norm_matmul_kernel(x_ref, nw_ref, w_ref, o_ref, hn_ref, *, sq_relu):
    @pl.when(pl.program_id(1) == 0)
    def _():
        hn_ref[...] = _rms_normalize(x_ref, nw_ref)

    acc = _dot(hn_ref[...], w_ref[...])
    if sq_relu:
        acc = jnp.square(jnp.maximum(acc, 0.0))
    o_ref[...] = acc.astype(o_ref.dtype)


def _norm_matmul(x, norm_w, w, layer, out_dtype, *, tm, tn, sq_relu=False):
    m, d = x.shape
    n = w.shape[2]
    tm, tn = min(tm, m), min(tn, n)
    assert n % tn == 0 and m % tm == 0
    return pl.pallas_call(
        functools.partial(_norm_matmul_kernel, sq_relu=sq_relu),
        grid=(m // tm, n // tn),
        in_specs=[pl.BlockSpec((tm, d), lambda i, j: (i, 0)),
                  pl.BlockSpec((None, 1, d), lambda i, j: (layer, 0, 0)),
                  pl.BlockSpec((None, d, tn), lambda i, j: (layer, 0, j))],
        out_specs=pl.BlockSpec((tm, tn), lambda i, j: (i, j)),
        out_shape=jax.ShapeDtypeStruct((m, n), out_dtype),
        scratch_shapes=[pltpu.VMEM((tm, d), BF16)],
        compiler_params=_cparams("parallel", "arbitrary"),
        name="norm_matmul",
    )(x, norm_w.reshape(norm_w.shape[0], 1, d), w)


IN_TILE = 768
SB_COLS = 3 * SB_WIDTH
PROJ_COLS = 3 * SC_WIDTH + 4 * GDN_WIDTH
IN_DIM = SB_COLS + PROJ_COLS + 2 * GDN_HEADS
SB_TILES = SB_COLS // IN_TILE
PROJ_TILES = PROJ_COLS // IN_TILE


def _in_proj_kernel(x_ref, nw_ref, w_ref, sb_ref, proj_ref, gate_ref, hn_ref):
    j = pl.program_id(1)

    @pl.when(j == 0)
    def _():
        hn_ref[...] = _rms_normalize(x_ref, nw_ref)
        w = w_ref[:, :HEAD_DIM]
        lane = lax.broadcasted_iota(jnp.int32, w.shape, 1)
        gate_ref[...] = _dot(hn_ref[...], jnp.where(lane < 2 * GDN_HEADS, w, jnp.zeros_like(w)))

    @pl.when(jnp.logical_and(j >= 1, j <= SB_TILES))
    def _():
        sb_ref[...] = _dot(hn_ref[...], w_ref[...]).astype(sb_ref.dtype)

    @pl.when(j > SB_TILES)
    def _():
        proj_ref[...] = _dot(hn_ref[...], w_ref[...])


def _in_proj(x, norm_w, w_in16, layer, *, tm):
    m, d = x.shape
    tm = min(tm, m)
    assert m % tm == 0 and w_in16.shape[2] == IN_DIM
    return pl.pallas_call(
        _in_proj_kernel,
        grid=(m // tm, SB_TILES + PROJ_TILES + 1),
        in_specs=[pl.BlockSpec((tm, d), lambda i, j: (i, 0)),
                  pl.BlockSpec((None, 1, d), lambda i, j: (layer, 0, 0)),
                  pl.BlockSpec((None, d, IN_TILE),
                               lambda i, j: (layer, 0, jnp.where(j == 0, SB_TILES + PROJ_TILES, j - 1)))],
        out_specs=[pl.BlockSpec((tm, IN_TILE), lambda i, j: (i, jnp.clip(j - 1, 0, SB_TILES - 1))),
                   pl.BlockSpec((tm, IN_TILE), lambda i, j: (i, jnp.clip(j - 1 - SB_TILES, 0, PROJ_TILES - 1))),
                   pl.BlockSpec((tm, HEAD_DIM), lambda i, j: (i, 0))],
        out_shape=[jax.ShapeDtypeStruct((m, SB_COLS), BF16),
                   jax.ShapeDtypeStruct((m, PROJ_COLS), F32),
                   jax.ShapeDtypeStruct((m, HEAD_DIM), F32)],
        scratch_shapes=[pltpu.VMEM((tm, d), BF16)],
        compiler_params=_cparams("parallel", "arbitrary"),
        name="in_proj",
    )(x, norm_w.reshape(norm_w.shape[0], 1, d), w_in16)


def _softplus(z):
    return jnp.maximum(z, 0.0) + jnp.log(1.0 + jnp.exp(-jnp.abs(z)))


LOG2_E = 1.4426950408889634
SB_ZERO_MASS_LOG2 = 160.0
SB_ROW_GROUPS = 2


def _softplus_log2(z2):
    return jnp.maximum(z2, 0.0) + jnp.log(1.0 + jnp.exp2(-jnp.abs(z2))) * LOG2_E


def _attn_kernel(q_ref, k_ref, v_ref, o_ref, *, blk, scale, heads):
    i = pl.program_id(2)
    row = lax.broadcasted_iota(jnp.int32, (blk, blk), 0)
    col = lax.broadcasted_iota(jnp.int32, (blk, blk), 1)
    suffix_ones = jnp.where(row >= col, 1.0, 0.0).astype(BF16)
    causal = col < row
    lanes = lambda hd: slice(hd * HEAD_DIM, (hd + 1) * HEAD_DIM)
    step = blk // SB_ROW_GROUPS
    groups = [(hd, slice(r * step, (r + 1) * step)) for hd in range(heads) for r in range(SB_ROW_GROUPS)]
    q = [q_ref[:, lanes(hd)] for hd in range(heads)]

    def suffix_sum(sp):
        hi = sp.astype(BF16)
        lo = (sp - hi.astype(F32)).astype(BF16)
        return _dot(hi, suffix_ones) + _dot(lo, suffix_ones)

    def key_block(ref, j, hd):
        return ref[pl.ds(pl.multiple_of(j * blk, blk), blk), lanes(hd)]

    def tile(blocks, carry, acc, diagonal_last, keep_first):
        n = len(blocks)
        kb = [jnp.concatenate([key_block(k_ref, j, hd) for j in blocks], axis=0) if n > 1
              else key_block(k_ref, blocks[0], hd) for hd in range(heads)]
        vb = [[key_block(v_ref, j, hd) for j in blocks] for hd in range(heads)]
        z2 = [_dot_nt(q[hd][g], kb[hd]) * (scale * LOG2_E) for hd, g in groups]
        sp = [_softplus_log2(z) for z in z2]
        carry = [carry[hd][g] for hd, g in groups]
        acc = [acc[hd][g] for hd, g in groups]
        for part in reversed(range(n)):
            cols = slice(part * blk, (part + 1) * blk)
            on_diagonal = diagonal_last and part == n - 1
            mass, w = [], []
            for x, (hd, g) in enumerate(groups):
                sp_part = jnp.where(causal[g], sp[x][:, cols], 0.0) if on_diagonal else sp[x][:, cols]
                mass.append(suffix_sum(sp_part) + carry[x])
                carry[x] = mass[x][:, 0:1]
            for x, (hd, g) in enumerate(groups):
                wx = jnp.exp2(z2[x][:, cols] - mass[x])
                w.append((jnp.where(causal[g], wx, 0.0) if on_diagonal else wx).astype(BF16))
            for x, (hd, g) in enumerate(groups):
                contrib = _dot(w[x], vb[hd][part])
                if part == 0 and keep_first is not None:
                    contrib = jnp.where(keep_first, contrib, 0.0)
                acc[x] = acc[x] + contrib
        per_head = lambda vals: [jnp.concatenate([v for v, (h2, _) in zip(vals, groups) if h2 == hd], axis=0)
                                 for hd in range(heads)]
        return per_head(carry), per_head(acc)

    def min_mass_of(carry):
        return functools.reduce(jnp.minimum, [jnp.min(c) for c in carry])

    prev = jnp.maximum(i - 1, 0)
    carry, acc = tile([prev, i], [jnp.zeros((blk, 1), F32)] * heads, [jnp.zeros((blk, HEAD_DIM), F32)] * heads,
                      True, i > 0)
    n_left = prev

    def cond(c):
        return jnp.logical_and(c[0] < n_left // 2, c[1] <= SB_ZERO_MASS_LOG2)

    def body(c):
        j = prev - 2 * (c[0] + 1)
        carry, acc = tile([j, j + 1], c[2], c[3], False, None)
        return c[0] + 1, min_mass_of(carry), carry, acc

    _, min_mass, carry, acc = lax.while_loop(cond, body, (jnp.int32(0), min_mass_of(carry), carry, acc))

    def last(c):
        return tile([0], c[0], c[1], False, None)

    odd_left = jnp.logical_and(n_left % 2 == 1, min_mass <= SB_ZERO_MASS_LOG2)
    carry, acc = lax.cond(odd_left, last, lambda c: c, (carry, acc))
    for hd in range(heads):
        o_ref[:, lanes(hd)] = acc[hd].astype(o_ref.dtype)


def _sb_attention(qkv, batch, seq, *, blk, heads_per_step):
    assert seq % (2 * blk) == 0 and SB_HEADS % heads_per_step == 0
    nq = seq // blk
    width = heads_per_step * HEAD_DIM
    hsteps = SB_HEADS // heads_per_step
    return pl.pallas_call(
        functools.partial(_attn_kernel, blk=blk, scale=HEAD_DIM ** -0.5, heads=heads_per_step),
        grid=(batch, hsteps, nq),
        in_specs=[pl.BlockSpec((blk, width), lambda b, hh, i: (b * nq + i, hh)),
                  pl.BlockSpec((seq, width), lambda b, hh, i: (b, hsteps + hh)),
                  pl.BlockSpec((seq, width), lambda b, hh, i: (b, 2 * hsteps + hh))],
        out_specs=pl.BlockSpec((blk, width), lambda b, hh, i: (b * nq + i, hh)),
        out_shape=jax.ShapeDtypeStruct((batch * seq, SB_WIDTH), BF16),
        compiler_params=_cparams("parallel", "parallel", "arbitrary"),
        name="sb_attention",
    )(qkv, qkv, qkv)


def _halo_index(rows_per_tile):
    per = rows_per_tile // HALO
    return lambda t: jnp.maximum(t * per - 1, 0)


def _sconv_kernel(b_ref, c_ref, h_ref, ch_ref, hh_ref, w_ref, o_ref, xs_ref, *, tm):
    first = pl.program_id(1) == 0
    halo = ch_ref[...] * hh_ref[...]
    xs_ref[0:HALO, :] = jnp.where(first, 0.0, halo)
    xs_ref[HALO:HALO + tm, :] = c_ref[...] * h_ref[...]
    conv = jnp.zeros((tm, SC_WIDTH), F32)
    for i in range(SC_CONV):
        conv = conv + w_ref[i:i + 1, :] * xs_ref[pl.ds(HALO - SC_CONV + 1 + i, tm), :]
    o_ref[...] = (b_ref[...] * conv).astype(o_ref.dtype)


def _short_conv(proj, conv_w, layer, batch, seq, *, tm):
    tm = min(tm, seq)
    nt = seq // tm
    hal = _halo_index(tm)
    tile = lambda cb: pl.BlockSpec((tm, SC_WIDTH), lambda b, i: (b * nt + i, cb))
    halo = lambda cb: pl.BlockSpec((HALO, SC_WIDTH), lambda b, i: (hal(b * nt + i), cb))
    return pl.pallas_call(
        functools.partial(_sconv_kernel, tm=tm),
        grid=(batch, nt),
        in_specs=[tile(0), tile(1), tile(2), halo(1), halo(2),
                  pl.BlockSpec((None, SC_CONV, SC_WIDTH), lambda b, i: (layer, 0, 0))],
        out_specs=pl.BlockSpec((tm, SC_WIDTH), lambda b, i: (b * nt + i, 0)),
        out_shape=jax.ShapeDtypeStruct((batch * seq, SC_WIDTH), BF16),
        scratch_shapes=[pltpu.VMEM((HALO + tm, SC_WIDTH), F32)],
        compiler_params=_cparams("parallel", "arbitrary"),
        name="short_conv",
    )(proj, proj, proj, proj, proj, conv_w)


def _gdn_kernel(q_ref, k_ref, v_ref, z_ref, qh_ref, kh_ref, vh_ref, g_ref,
                cwq_ref, cwk_ref, cwv_ref, alog_ref, dtb_ref, nw_ref,
                o_ref, state_ref, xs_ref, *, chunks):
    c = GDN_CHUNK
    rows = chunks * c
    first = pl.program_id(1) == 0

    @pl.when(first)
    def _():
        state_ref[...] = jnp.zeros_like(state_ref)

    def conv_silu(slot, x_ref, halo_ref, w_ref):
        xs_ref[slot, 0:HALO, :] = jnp.where(first, 0.0, halo_ref[...])
        xs_ref[slot, HALO:HALO + rows, :] = x_ref[...]
        x = xs_ref[slot]
        y = w_ref[GDN_CONV - 1:GDN_CONV, :] * x[HALO:]
        for back in range(1, GDN_CONV):
            tap = w_ref[GDN_CONV - 1 - back:GDN_CONV - back, :]
            y = y + tap * pltpu.roll(x, back, axis=0)[HALO:]
        return y * jax.nn.sigmoid(y)

    q_all = conv_silu(0, q_ref, qh_ref, cwq_ref)
    k_all = conv_silu(1, k_ref, kh_ref, cwk_ref)
    v_all = conv_silu(2, v_ref, vh_ref, cwv_ref)

    gates = g_ref[...]
    g_all = -jnp.exp(alog_ref[...]) * _softplus(gates + dtb_ref[...])
    beta_all = jax.nn.sigmoid(gates)

    row = lax.broadcasted_iota(jnp.int32, (c, c), 0)
    col = lax.broadcasted_iota(jnp.int32, (c, c), 1)
    tri = col <= row
    strict = col < row
    lower_ones = jnp.where(tri, 1.0, 0.0)
    eye = jnp.where(row == col, 1.0, 0.0)

    heads = range(GDN_HEADS)
    pairs = [(ch, h) for ch in range(chunks) for h in heads]
    rs = lambda ch: slice(ch * c, (ch + 1) * c)
    hs = lambda h: slice(h * HEAD_DIM, (h + 1) * HEAD_DIM)

    gcum_col = [_dot(lower_ones, g_all[rs(ch)]) for ch in range(chunks)]
    gcum_row = [_dot_nt(g_all[rs(ch)].T, lower_ones) for ch in range(chunks)]

    q, k, v, beta, g_c, g_last, e_g, decay = {}, {}, {}, {}, {}, {}, {}, {}
    for p in pairs:
        ch, h = p
        qq = q_all[rs(ch), hs(h)]
        kk = k_all[rs(ch), hs(h)]
        q[p] = qq * lax.rsqrt(jnp.sum(qq * qq, axis=-1, keepdims=True) + L2_EPS) * (HEAD_DIM ** -0.5)
        k[p] = kk * lax.rsqrt(jnp.sum(kk * kk, axis=-1, keepdims=True) + L2_EPS)
        v[p] = v_all[rs(ch), hs(h)]
        beta[p] = beta_all[rs(ch), GDN_HEADS + h:GDN_HEADS + h + 1]
        g_c[p] = gcum_col[ch][:, h:h + 1]
        g_r = gcum_row[ch][h:h + 1, :]
        g_last[p] = g_c[p][c - 1:c, :]
        e_g[p] = jnp.exp(g_c[p])
        decay[p] = jnp.where(tri, jnp.exp(jnp.where(tri, g_c[p] - g_r, 0.0)), 0.0)

    k16 = {p: k[p].astype(BF16) for p in pairs}
    kk = {p: _dot_nt(k16[p], k16[p]) for p in pairs}
    qk = {p: _dot_nt(q[p].astype(BF16), k16[p]) for p in pairs}
    attn = {p: jnp.where(tri, qk[p] * decay[p], 0.0).astype(BF16) for p in pairs}
    pw = {p: jnp.where(strict, -(beta[p] * kk[p] * decay[p]), 0.0) for p in pairs}
    t_mat = {p: eye + pw[p] for p in pairs}
    for _ in range(5):
        pw = {p: _dot(pw[p], pw[p]) for p in pairs}
        t_mat = {p: t_mat[p] + _dot(t_mat[p], pw[p]) for p in pairs}
    rhs = {p: jnp.concatenate([v[p] * beta[p], k[p] * (beta[p] * e_g[p])], axis=1).astype(BF16) for p in pairs}
    uw = {p: _dot(t_mat[p].astype(BF16), rhs[p]) for p in pairs}
    wq = {p: jnp.concatenate([uw[p][:, HEAD_DIM:], q[p] * e_g[p]], axis=0).astype(BF16) for p in pairs}
    k_dec = {p: (k[p] * jnp.exp(g_last[p] - g_c[p])).astype(BF16) for p in pairs}

    state = {h: state_ref[h] for h in heads}
    out = {}
    for ch in range(chunks):
        s16 = {h: state[h].astype(BF16) for h in heads}
        ws = {h: _dot(wq[ch, h], s16[h]) for h in heads}
        v_new = {h: (uw[ch, h][:, :HEAD_DIM] - ws[h][:c]).astype(BF16) for h in heads}
        out.update({(ch, h): ws[h][c:] + _dot(attn[ch, h], v_new[h]) for h in heads})
        state = {h: state[h] * jnp.exp(g_last[ch, h]) + _dot_tn(k_dec[ch, h], v_new[h]) for h in heads}
    for h in heads:
        state_ref[h] = state[h]

    for p in pairs:
        ch, h = p
        o = out[p]
        r = lax.rsqrt(jnp.mean(o * o, axis=-1, keepdims=True) + RMS_EPS)
        zz = z_ref[rs(ch), hs(h)]
        y = (o * r * nw_ref[...]) * (zz * jax.nn.sigmoid(zz))
        o_ref[rs(ch), hs(h)] = y.astype(o_ref.dtype)


def _gdn_mixer(proj, gates, conv_w, a_log, dt_bias, norm_w, layer, batch, seq, col0, *, chunks_per_step):
    rows = chunks_per_step * GDN_CHUNK
    assert seq % rows == 0
    nt = seq // rows
    cb0 = col0 // GDN_WIDTH
    hal = _halo_index(rows)
    tile = lambda cb: pl.BlockSpec((rows, GDN_WIDTH), lambda b, i: (b * nt + i, cb0 + cb))
    halo = lambda cb: pl.BlockSpec((HALO, GDN_WIDTH), lambda b, i: (hal(b * nt + i), cb0 + cb))
    cw = lambda cb: pl.BlockSpec((None, GDN_CONV, GDN_WIDTH), lambda b, i: (layer, 0, cb))
    rowvec = pl.BlockSpec((None, 1, HEAD_DIM), lambda b, i: (layer, 0, 0))
    return pl.pallas_call(
        functools.partial(_gdn_kernel, chunks=chunks_per_step),
        grid=(batch, nt),
        in_specs=[tile(0), tile(1), tile(2), tile(3), halo(0), halo(1), halo(2),
                  pl.BlockSpec((rows, HEAD_DIM), lambda b, i: (b * nt + i, 0)),
                  cw(0), cw(1), cw(2), rowvec, rowvec, rowvec],
        out_specs=pl.BlockSpec((rows, GDN_WIDTH), lambda b, i: (b * nt + i, 0)),
        out_shape=jax.ShapeDtypeStruct((batch * seq, GDN_WIDTH), BF16),
        scratch_shapes=[pltpu.VMEM((GDN_HEADS, HEAD_DIM, HEAD_DIM), F32),
                        pltpu.VMEM((3, HALO + rows, GDN_WIDTH), F32)],
        compiler_params=_cparams("parallel", "arbitrary"),
        name="gdn_mixer",
    )(proj, proj, proj, proj, proj, proj, proj, gates,
      conv_w, conv_w, conv_w, a_log, dt_bias, norm_w)


def _out_proj_kernel(ysb_ref, ysc_ref, ygdn_ref, w_ref, res_ref, o_ref):
    acc = res_ref[...]
    acc = acc + _dot(ysb_ref[...], w_ref[0:SB_WIDTH, :])
    acc = acc + _dot(ysc_ref[...], w_ref[SB_WIDTH:SB_WIDTH + SC_WIDTH, :])
    acc = acc + _dot(ygdn_ref[...], w_ref[SB_WIDTH + SC_WIDTH:, :])
    o_ref[...] = acc


def _out_proj(y_sb, y_sc, y_gdn, w, layer, res, *, tm):
    m, d = res.shape
    kdim = w.shape[1]
    tm = min(tm, m)
    assert m % tm == 0
    rows = lambda width: pl.BlockSpec((tm, width), lambda i: (i, 0))
    return pl.pallas_call(
        _out_proj_kernel,
        grid=(m // tm,),
        in_specs=[rows(SB_WIDTH), rows(SC_WIDTH), rows(GDN_WIDTH),
                  pl.BlockSpec((None, kdim, d), lambda i: (layer, 0, 0)),
                  rows(d)],
        out_specs=rows(d),
        out_shape=jax.ShapeDtypeStruct((m, d), F32),
        compiler_params=_cparams("parallel"),
        name="out_proj",
    )(y_sb, y_sc, y_gdn, w, res)


def _matmul_res_kernel(a_ref, w_ref, res_ref, nw_ref, o_ref, *, final_norm):
    k = pl.program_id(2)

    @pl.when(k == 0)
    def _():
        o_ref[...] = res_ref[...]

    o_ref[...] += _dot(a_ref[...], w_ref[...])

    if final_norm:
        @pl.when(k == pl.num_programs(2) - 1)
        def _():
            x = o_ref[...]
            r = lax.rsqrt(jnp.mean(x * x, axis=-1, keepdims=True) + RMS_EPS)
            o_ref[...] = x * r * nw_ref[...]


def _matmul_res(a, w, layer, res, norm_w, *, tm, tn, tk, final_norm):
    m, kdim = a.shape
    n = w.shape[2]
    tm, tn, tk = min(tm, m), min(tn, n), min(tk, kdim)
    assert m % tm == 0 and n % tn == 0 and kdim % tk == 0 and (tn == n or not final_norm)
    return pl.pallas_call(
        functools.partial(_matmul_res_kernel, final_norm=final_norm),
        grid=(m // tm, n // tn, kdim // tk),
        in_specs=[pl.BlockSpec((tm, tk), lambda i, j, k: (i, k)),
                  pl.BlockSpec((None, tk, tn), lambda i, j, k: (layer, k, j)),
                  pl.BlockSpec((tm, tn), lambda i, j, k: (i, j)),
                  pl.BlockSpec((1, tn), lambda i, j, k: (0, j))],
        out_specs=pl.BlockSpec((tm, tn), lambda i, j, k: (i, j)),
        out_shape=jax.ShapeDtypeStruct((m, n), F32),
        compiler_params=_cparams("parallel", "parallel", "arbitrary"),
        name="matmul_res",
    )(a, w, res, norm_w.reshape(1, n))


def kernel(x, norm1_w, w_in, sc_conv_w, gdn_conv_w, gdn_a_log, gdn_dt_bias, gdn_norm_w, w_out, norm2_w,
           w_up, w_down, final_norm_w):
    batch, seq, d = x.shape
    layers = norm1_w.shape[0]
    w_in16, w_out16, w_up16, w_down16 = (w.astype(BF16) for w in (w_in, w_out, w_up, w_down))
    pad_heads = lambda a: jnp.zeros((layers, 1, HEAD_DIM), F32).at[:, 0, :GDN_HEADS].set(a)
    a_log, dt_bias = pad_heads(gdn_a_log), pad_heads(gdn_dt_bias)
    gdn_nw = gdn_norm_w.reshape(layers, 1, HEAD_DIM)

    h = x.reshape(batch * seq, d)
    for l in range(layers):
        qkv_sb, proj, gates = _in_proj(h, norm1_w, w_in16, l, tm=1024)
        y_sb = _sb_attention(qkv_sb, batch, seq, blk=256, heads_per_step=6)
        y_sc = _short_conv(proj, sc_conv_w, l, batch, seq, tm=512)
        y_gdn = _gdn_mixer(proj, gates, gdn_conv_w, a_log, dt_bias, gdn_nw, l, batch, seq, 3 * SC_WIDTH,
                           chunks_per_step=4)
        h = _out_proj(y_sb, y_sc, y_gdn, w_out16, l, h, tm=512)
        hidden = _norm_matmul(h, norm2_w, w_up16, l, BF16, tm=1024, tn=1024, sq_relu=True)
        last = l == layers - 1
        h = _matmul_res(hidden, w_down16, l, h, final_norm_w, tm=512 if last else 1024, tn=d if last else 1024,
                        tk=2048, final_norm=last)
    return h.reshape(batch, seq, d)
```

```python
import functools

import jax
import jax.numpy as jnp
from jax import lax
from jax.experimental import pallas as pl
from jax.experimental.pallas import tpu as pltpu

HEAD_DIM = 128
SB_HEADS = 6
SB_WIDTH = SB_HEADS * HEAD_DIM
SC_WIDTH = 4 * HEAD_DIM
GDN_HEADS = 6
GDN_WIDTH = GDN_HEADS * HEAD_DIM
SC_CONV = 3
GDN_CONV = 4
GDN_CHUNK = 64
RMS_EPS = 1e-6
L2_EPS = 1e-6
HALO = 8
VMEM_LIMIT_BYTES = 56 * 1024 * 1024

F32 = jnp.float32
BF16 = jnp.bfloat16


def _cparams(*sem):
    return pltpu.CompilerParams(dimension_semantics=sem, vmem_limit_bytes=VMEM_LIMIT_BYTES)


def _dot(a, b):
    return jnp.dot(a, b, preferred_element_type=F32)


def _dot_nt(a, b):
    return lax.dot_general(a, b, (((1,), (1,)), ((), ())), preferred_element_type=F32)


def _dot_tn(a, b):
    return lax.dot_general(a, b, (((0,), (0,)), ((), ())), preferred_element_type=F32)


def _rms_normalize(x_ref, nw_ref):
    x = x_ref[...]
    r = lax.rsqrt(jnp.mean(x * x, axis=-1, keepdims=True) + RMS_EPS)
    return (x * r * nw_ref[...]).astype(BF16)


def _norm_matmul_kernel(x_ref, nw_ref, w_ref, o_ref, hn_ref, *, sq_relu):
    @pl.when(pl.program_id(1) == 0)
    def _():
        hn_ref[...] = _rms_normalize(x_ref, nw_ref)

    acc = _dot(hn_ref[...], w_ref[...])
    if sq_relu:
        acc = jnp.square(jnp.maximum(acc, 0.0))
    o_ref[...] = acc.astype(o_ref.dtype)


def _norm_matmul(x, norm_w, w, layer, out_dtype, *, tm, tn, sq_relu=False):
    m, d = x.shape
    n = w.shape[2]
    tm, tn = min(tm, m), min(tn, n)
    assert n % tn == 0 and m % tm == 0
    return pl.pallas_call(
        functools.partial(_norm_matmul_kernel, sq_relu=sq_relu),
        grid=(m // tm, n // tn),
        in_specs=[pl.BlockSpec((tm, d), lambda i, j: (i, 0)),
                  pl.BlockSpec((None, 1, d), lambda i, j: (layer, 0, 0)),
                  pl.BlockSpec((None, d, tn), lambda i, j: (layer, 0, j))],
        out_specs=pl.BlockSpec((tm, tn), lambda i, j: (i, j)),
        out_shape=jax.ShapeDtypeStruct((m, n), out_dtype),
        scratch_shapes=[pltpu.VMEM((tm, d), BF16)],
        compiler_params=_cparams("parallel", "arbitrary"),
        name="norm_matmul",
    )(x, norm_w.reshape(norm_w.shape[0], 1, d), w)


IN_TILE = 768
SB_COLS = 3 * SB_WIDTH
PROJ_COLS = 3 * SC_WIDTH + 4 * GDN_WIDTH
IN_DIM = SB_COLS + PROJ_COLS + 2 * GDN_HEADS
SB_TILES = SB_COLS // IN_TILE
PROJ_TILES = PROJ_COLS // IN_TILE


def _in_proj_kernel(x_ref, nw_ref, w_ref, sb_ref, proj_ref, gate_ref, hn_ref):
    j = pl.program_id(1)

    @pl.when(j == 0)
    def _():
        hn_ref[...] = _rms_normalize(x_ref, nw_ref)
        w = w_ref[:, :HEAD_DIM]
        lane = lax.broadcasted_iota(jnp.int32, w.shape, 1)
        gate_ref[...] = _dot(hn_ref[...], jnp.where(lane < 2 * GDN_HEADS, w, jnp.zeros_like(w)))

    @pl.when(jnp.logical_and(j >= 1, j <= SB_TILES))
    def _():
        sb_ref[...] = _dot(hn_ref[...], w_ref[...]).astype(sb_ref.dtype)

    @pl.when(j > SB_TILES)
    def _():
        proj_ref[...] = _dot(hn_ref[...], w_ref[...])


def _in_proj(x, norm_w, w_in16, layer, *, tm):
    m, d = x.shape
    tm = min(tm, m)
    assert m % tm == 0 and w_in16.shape[2] == IN_DIM
    return pl.pallas_call(
        _in_proj_kernel,
        grid=(m // tm, SB_TILES + PROJ_TILES + 1),
        in_specs=[pl.BlockSpec((tm, d), lambda i, j: (i, 0)),
                  pl.BlockSpec((None, 1, d), lambda i, j: (layer, 0, 0)),
                  pl.BlockSpec((None, d, IN_TILE),
                               lambda i, j: (layer, 0, jnp.where(j == 0, SB_TILES + PROJ_TILES, j - 1)))],
        out_specs=[pl.BlockSpec((tm, IN_TILE), lambda i, j: (i, jnp.clip(j - 1, 0, SB_TILES - 1))),
                   pl.BlockSpec((tm, IN_TILE), lambda i, j: (i, jnp.clip(j - 1 - SB_TILES, 0, PROJ_TILES - 1))),
                   pl.BlockSpec((tm, HEAD_DIM), lambda i, j: (i, 0))],
        out_shape=[jax.ShapeDtypeStruct((m, SB_COLS), BF16),
                   jax.ShapeDtypeStruct((m, PROJ_COLS), F32),
                   jax.ShapeDtypeStruct((m, HEAD_DIM), F32)],
        scratch_shapes=[pltpu.VMEM((tm, d), BF16)],
        compiler_params=_cparams("parallel", "arbitrary"),
        name="in_proj",
    )(x, norm_w.reshape(norm_w.shape[0], 1, d), w_in16)


def _softplus(z):
    return jnp.maximum(z, 0.0) + jnp.log(1.0 + jnp.exp(-jnp.abs(z)))


LOG2_E = 1.4426950408889634
SB_ZERO_MASS_LOG2 = 160.0
SB_ROW_GROUPS = 2


def _softplus_log2(z2):
    return jnp.maximum(z2, 0.0) + jnp.log(1.0 + jnp.exp2(-jnp.abs(z2))) * LOG2_E


def _attn_kernel(q_ref, k_ref, v_ref, o_ref, *, blk, scale, heads):
    i = pl.program_id(2)
    row = lax.broadcasted_iota(jnp.int32, (blk, blk), 0)
    col = lax.broadcasted_iota(jnp.int32, (blk, blk), 1)
    suffix_ones = jnp.where(row >= col, 1.0, 0.0).astype(BF16)
    causal = col < row
    lanes = lambda hd: slice(hd * HEAD_DIM, (hd + 1) * HEAD_DIM)
    step = blk // SB_ROW_GROUPS
    groups = [(hd, slice(r * step, (r + 1) * step)) for hd in range(heads) for r in range(SB_ROW_GROUPS)]
    q = [q_ref[:, lanes(hd)] for hd in range(heads)]

    def suffix_sum(sp):
        hi = sp.astype(BF16)
        lo = (sp - hi.astype(F32)).astype(BF16)
        return _dot(hi, suffix_ones) + _dot(lo, suffix_ones)

    def key_block(ref, j, hd):
        return ref[pl.ds(pl.multiple_of(j * blk, blk), blk), lanes(hd)]

    def tile(blocks, carry, acc, diagonal_last, keep_first):
        n = len(blocks)
        kb = [jnp.concatenate([key_block(k_ref, j, hd) for j in blocks], axis=0) if n > 1
              else key_block(k_ref, blocks[0], hd) for hd in range(heads)]
        vb = [[key_block(v_ref, j, hd) for j in blocks] for hd in range(heads)]
        z2 = [_dot_nt(q[hd][g], kb[hd]) * (scale * LOG2_E) for hd, g in groups]
        sp = [_softplus_log2(z) for z in z2]
        carry = [carry[hd][g] for hd, g in groups]
        acc = [acc[hd][g] for hd, g in groups]
        for part in reversed(range(n)):
            cols = slice(part * blk, (part + 1) * blk)
            on_diagonal = diagonal_last and part == n - 1
            mass, w = [], []
            for x, (hd, g) in enumerate(groups):
                sp_part = jnp.where(causal[g], sp[x][:, cols], 0.0) if on_diagonal else sp[x][:, cols]
                mass.append(suffix_sum(sp_part) + carry[x])
                carry[x] = mass[x][:, 0:1]
            for x, (hd, g) in enumerate(groups):
                wx = jnp.exp2(z2[x][:, cols] - mass[x])
                w.append((jnp.where(causal[g], wx, 0.0) if on_diagonal else wx).astype(BF16))
            for x, (hd, g) in enumerate(groups):
                contrib = _dot(w[x], vb[hd][part])
                if part == 0 and keep_first is not None:
                    contrib = jnp.where(keep_first, contrib, 0.0)
                acc[x] = acc[x] + contrib
        per_head = lambda vals: [jnp.concatenate([v for v, (h2, _) in zip(vals, groups) if h2 == hd], axis=0)
                                 for hd in range(heads)]
        return per_head(carry), per_head(acc)

    def min_mass_of(carry):
        return functools.reduce(jnp.minimum, [jnp.min(c) for c in carry])

    prev = jnp.maximum(i - 1, 0)
    carry, acc = tile([prev, i], [jnp.zeros((blk, 1), F32)] * heads, [jnp.zeros((blk, HEAD_DIM), F32)] * heads,
                      True, i > 0)
    n_left = prev

    def cond(c):
        return jnp.logical_and(c[0] < n_left // 2, c[1] <= SB_ZERO_MASS_LOG2)

    def body(c):
        j = prev - 2 * (c[0] + 1)
        carry, acc = tile([j, j + 1], c[2], c[3], False, None)
        return c[0] + 1, min_mass_of(carry), carry, acc

    _, min_mass, carry, acc = lax.while_loop(cond, body, (jnp.int32(0), min_mass_of(carry), carry, acc))

    def last(c):
        return tile([0], c[0], c[1], False, None)

    odd_left = jnp.logical_and(n_left % 2 == 1, min_mass <= SB_ZERO_MASS_LOG2)
    carry, acc = lax.cond(odd_left, last, lambda c: c, (carry, acc))
    for hd in range(heads):
        o_ref[:, lanes(hd)] = acc[hd].astype(o_ref.dtype)


def _sb_attention(qkv, batch, seq, *, blk, heads_per_step):
    assert seq % (2 * blk) == 0 and SB_HEADS % heads_per_step == 0
    nq = seq // blk
    width = heads_per_step * HEAD_DIM
    hsteps = SB_HEADS // heads_per_step
    return pl.pallas_call(
        functools.partial(_attn_kernel, blk=blk, scale=HEAD_DIM ** -0.5, heads=heads_per_step),
        grid=(batch, hsteps, nq),
        in_specs=[pl.BlockSpec((blk, width), lambda b, hh, i: (b * nq + i, hh)),
                  pl.BlockSpec((seq, width), lambda b, hh, i: (b, hsteps + hh)),
                  pl.BlockSpec((seq, width), lambda b, hh, i: (b, 2 * hsteps + hh))],
        out_specs=pl.BlockSpec((blk, width), lambda b, hh, i: (b * nq + i, hh)),
        out_shape=jax.ShapeDtypeStruct((batch * seq, SB_WIDTH), BF16),
        compiler_params=_cparams("parallel", "parallel", "arbitrary"),
        name="sb_attention",
    )(qkv, qkv, qkv)


def _halo_index(rows_per_tile):
    per = rows_per_tile // HALO
    return lambda t: jnp.maximum(t * per - 1, 0)


def _causal_conv(x_ext, w_ref, taps):
    y = w_ref[taps - 1:taps, :] * x_ext[HALO:]
    for back in range(1, taps):
        y = y + w_ref[taps - 1 - back:taps - back, :] * pltpu.roll(x_ext, back, axis=0)[HALO:]
    return y


def _gdn_kernel(q_ref, k_ref, v_ref, z_ref, qh_ref, kh_ref, vh_ref, g_ref,
                cwq_ref, cwk_ref, cwv_ref, alog_ref, dtb_ref, nw_ref,
                o_ref, state_ref, xs_ref, *, chunks):
    c = GDN_CHUNK
    rows = chunks * c
    first = pl.program_id(1) == 0

    @pl.when(first)
    def _():
        state_ref[...] = jnp.zeros_like(state_ref)

    def conv_silu(slot, x_ref, halo_ref, w_ref):
        xs_ref[slot, 0:HALO, :] = jnp.where(first, 0.0, halo_ref[...])
        xs_ref[slot, HALO:HALO + rows, :] = x_ref[...]
        y = _causal_conv(xs_ref[slot], w_ref, GDN_CONV)
        return y * jax.nn.sigmoid(y)

    q_all = conv_silu(0, q_ref, qh_ref, cwq_ref)
    k_all = conv_silu(1, k_ref, kh_ref, cwk_ref)
    v_all = conv_silu(2, v_ref, vh_ref, cwv_ref)

    gates = g_ref[...]
    g_all = -jnp.exp(alog_ref[...]) * _softplus(gates + dtb_ref[...])
    beta_all = jax.nn.sigmoid(gates)

    row = lax.broadcasted_iota(jnp.int32, (c, c), 0)
    col = lax.broadcasted_iota(jnp.int32, (c, c), 1)
    tri = col <= row
    strict = col < row
    lower_ones = jnp.where(tri, 1.0, 0.0)
    eye = jnp.where(row == col, 1.0, 0.0)

    heads = range(GDN_HEADS)
    pairs = [(ch, h) for ch in range(chunks) for h in heads]
    rs = lambda ch: slice(ch * c, (ch + 1) * c)
    hs = lambda h: slice(h * HEAD_DIM, (h + 1) * HEAD_DIM)

    gcum_col = [_dot(lower_ones, g_all[rs(ch)]) for ch in range(chunks)]
    gcum_row = [_dot_nt(g_all[rs(ch)].T, lower_ones) for ch in range(chunks)]

    q, k, v, beta, g_c, g_last, e_g, decay = {}, {}, {}, {}, {}, {}, {}, {}
    for p in pairs:
        ch, h = p
        qq = q_all[rs(ch), hs(h)]
        kk = k_all[rs(ch), hs(h)]
        q[p] = qq * lax.rsqrt(jnp.sum(qq * qq, axis=-1, keepdims=True) + L2_EPS) * (HEAD_DIM ** -0.5)
        k[p] = kk * lax.rsqrt(jnp.sum(kk * kk, axis=-1, keepdims=True) + L2_EPS)
        v[p] = v_all[rs(ch), hs(h)]
        beta[p] = beta_all[rs(ch), GDN_HEADS + h:GDN_HEADS + h + 1]
        g_c[p] = gcum_col[ch][:, h:h + 1]
        g_r = gcum_row[ch][h:h + 1, :]
        g_last[p] = g_c[p][c - 1:c, :]
        e_g[p] = jnp.exp(g_c[p])
        decay[p] = jnp.where(tri, jnp.exp(jnp.where(tri, g_c[p] - g_r, 0.0)), 0.0)

    k16 = {p: k[p].astype(BF16) for p in pairs}
    kk = {p: _dot_nt(k16[p], k16[p]) for p in pairs}
    qk = {p: _dot_nt(q[p].astype(BF16), k16[p]) for p in pairs}
    attn = {p: jnp.where(tri, qk[p] * decay[p], 0.0).astype(BF16) for p in pairs}
    pw = {p: jnp.where(strict, -(beta[p] * kk[p] * decay[p]), 0.0) for p in pairs}
    t_mat = {p: eye + pw[p] for p in pairs}
    for _ in range(5):
        pw = {p: _dot(pw[p], pw[p]) for p in pairs}
        t_mat = {p: t_mat[p] + _dot(t_mat[p], pw[p]) for p in pairs}
    rhs = {p: jnp.concatenate([v[p] * beta[p], k[p] * (beta[p] * e_g[p])], axis=1).astype(BF16) for p in pairs}
    uw = {p: _dot(t_mat[p].astype(BF16), rhs[p]) for p in pairs}
    wq = {p: jnp.concatenate([uw[p][:, HEAD_DIM:], q[p] * e_g[p]], axis=0).astype(BF16) for p in pairs}
    k_dec = {p: (k[p] * jnp.exp(g_last[p] - g_c[p])).astype(BF16) for p in pairs}

    state = {h: state_ref[h] for h in heads}
    out = {}
    for ch in range(chunks):
        s16 = {h: state[h].astype(BF16) for h in heads}
        ws = {h: _dot(wq[ch, h], s16[h]) for h in heads}
        v_new = {h: (uw[ch, h][:, :HEAD_DIM] - ws[h][:c]).astype(BF16) for h in heads}
        out.update({(ch, h): ws[h][c:] + _dot(attn[ch, h], v_new[h]) for h in heads})
        state = {h: state[h] * jnp.exp(g_last[ch, h]) + _dot_tn(k_dec[ch, h], v_new[h]) for h in heads}
    for h in heads:
        state_ref[h] = state[h]

    for p in pairs:
        ch, h = p
        o = out[p]
        r = lax.rsqrt(jnp.mean(o * o, axis=-1, keepdims=True) + RMS_EPS)
        zz = z_ref[rs(ch), hs(h)]
        y = (o * r * nw_ref[...]) * (zz * jax.nn.sigmoid(zz))
        o_ref[rs(ch), hs(h)] = y.astype(o_ref.dtype)


def _gdn_mixer(proj, gates, conv_w, a_log, dt_bias, norm_w, layer, batch, seq, col0, *, chunks_per_step):
    rows = chunks_per_step * GDN_CHUNK
    assert seq % rows == 0
    nt = seq // rows
    cb0 = col0 // GDN_WIDTH
    hal = _halo_index(rows)
    tile = lambda cb: pl.BlockSpec((rows, GDN_WIDTH), lambda b, i: (b * nt + i, cb0 + cb))
    halo = lambda cb: pl.BlockSpec((HALO, GDN_WIDTH), lambda b, i: (hal(b * nt + i), cb0 + cb))
    cw = lambda cb: pl.BlockSpec((None, GDN_CONV, GDN_WIDTH), lambda b, i: (layer, 0, cb))
    rowvec = pl.BlockSpec((None, 1, HEAD_DIM), lambda b, i: (layer, 0, 0))
    return pl.pallas_call(
        functools.partial(_gdn_kernel, chunks=chunks_per_step),
        grid=(batch, nt),
        in_specs=[tile(0), tile(1), tile(2), tile(3), halo(0), halo(1), halo(2),
                  pl.BlockSpec((rows, HEAD_DIM), lambda b, i: (b * nt + i, 0)),
                  cw(0), cw(1), cw(2), rowvec, rowvec, rowvec],
        out_specs=pl.BlockSpec((rows, GDN_WIDTH), lambda b, i: (b * nt + i, 0)),
        out_shape=jax.ShapeDtypeStruct((batch * seq, GDN_WIDTH), BF16),
        scratch_shapes=[pltpu.VMEM((GDN_HEADS, HEAD_DIM, HEAD_DIM), F32),
                        pltpu.VMEM((3, HALO + rows, GDN_WIDTH), F32)],
        compiler_params=_cparams("parallel", "arbitrary"),
        name="gdn_mixer",
    )(proj, proj, proj, proj, proj, proj, proj, gates,
      conv_w, conv_w, conv_w, a_log, dt_bias, norm_w)


def _out_proj_kernel(ysb_ref, b_ref, c_ref, h_ref, ch_ref, hh_ref, cw_ref, ygdn_ref, w_ref, res_ref, o_ref,
                     xs_ref, *, tiles_per_seq):
    tm = b_ref.shape[0]
    first = pl.program_id(0) % tiles_per_seq == 0
    xs_ref[0:HALO, :] = jnp.where(first, 0.0, ch_ref[...] * hh_ref[...])
    xs_ref[HALO:HALO + tm, :] = c_ref[...] * h_ref[...]
    y_sc = (b_ref[...] * _causal_conv(xs_ref[...], cw_ref, SC_CONV)).astype(BF16)
    acc = res_ref[...]
    acc = acc + _dot(ysb_ref[...], w_ref[0:SB_WIDTH, :])
    acc = acc + _dot(y_sc, w_ref[SB_WIDTH:SB_WIDTH + SC_WIDTH, :])
    acc = acc + _dot(ygdn_ref[...], w_ref[SB_WIDTH + SC_WIDTH:, :])
    o_ref[...] = acc


def _out_proj(y_sb, proj, sc_conv_w, y_gdn, w, layer, res, seq, *, tm):
    m, d = res.shape
    kdim = w.shape[1]
    tm = min(tm, seq)
    assert seq % tm == 0 and m % seq == 0
    hal = _halo_index(tm)
    rows = lambda width, cb=0: pl.BlockSpec((tm, width), lambda i: (i, cb))
    halo = lambda cb: pl.BlockSpec((HALO, SC_WIDTH), lambda i: (hal(i), cb))
    return pl.pallas_call(
        functools.partial(_out_proj_kernel, tiles_per_seq=seq // tm),
        grid=(m // tm,),
        in_specs=[rows(SB_WIDTH), rows(SC_WIDTH, 0), rows(SC_WIDTH, 1), rows(SC_WIDTH, 2), halo(1), halo(2),
                  pl.BlockSpec((None, SC_CONV, SC_WIDTH), lambda i: (layer, 0, 0)),
                  rows(GDN_WIDTH),
                  pl.BlockSpec((None, kdim, d), lambda i: (layer, 0, 0)),
                  rows(d)],
        out_specs=rows(d),
        out_shape=jax.ShapeDtypeStruct((m, d), F32),
        scratch_shapes=[pltpu.VMEM((HALO + tm, SC_WIDTH), F32)],
        compiler_params=_cparams("parallel"),
        name="out_proj",
    )(y_sb, proj, proj, proj, proj, proj, sc_conv_w, y_gdn, w, res)


def _matmul_res_kernel(a_ref, w_ref, res_ref, nw_ref, o_ref, *, final_norm):
    k = pl.program_id(2)

    @pl.when(k == 0)
    def _():
        o_ref[...] = res_ref[...]

    o_ref[...] += _dot(a_ref[...], w_ref[...])

    if final_norm:
        @pl.when(k == pl.num_programs(2) - 1)
        def _():
            x = o_ref[...]
            r = lax.rsqrt(jnp.mean(x * x, axis=-1, keepdims=True) + RMS_EPS)
            o_ref[...] = x * r * nw_ref[...]


def _matmul_res(a, w, layer, res, norm_w, *, tm, tn, tk, final_norm):
    m, kdim = a.shape
    n = w.shape[2]
    tm, tn, tk = min(tm, m), min(tn, n), min(tk, kdim)
    assert m % tm == 0 and n % tn == 0 and kdim % tk == 0 and (tn == n or not final_norm)
    return pl.pallas_call(
        functools.partial(_matmul_res_kernel, final_norm=final_norm),
        grid=(m // tm, n // tn, kdim // tk),
        in_specs=[pl.BlockSpec((tm, tk), lambda i, j, k: (i, k)),
                  pl.BlockSpec((None, tk, tn), lambda i, j, k: (layer, k, j)),
                  pl.BlockSpec((tm, tn), lambda i, j, k: (i, j)),
                  pl.BlockSpec((1, tn), lambda i, j, k: (0, j))],
        out_specs=pl.BlockSpec((tm, tn), lambda i, j, k: (i, j)),
        out_shape=jax.ShapeDtypeStruct((m, n), F32),
        compiler_params=_cparams("parallel", "parallel", "arbitrary"),
        name="matmul_res",
    )(a, w, res, norm_w.reshape(1, n))


def kernel(x, norm1_w, w_in, sc_conv_w, gdn_conv_w, gdn_a_log, gdn_dt_bias, gdn_norm_w, w_out, norm2_w,
           w_up, w_down, final_norm_w):
    batch, seq, d = x.shape
    layers = norm1_w.shape[0]
    w_in16, w_out16, w_up16, w_down16 = (w.astype(BF16) for w in (w_in, w_out, w_up, w_down))
    pad_heads = lambda a: jnp.zeros((layers, 1, HEAD_DIM), F32).at[:, 0, :GDN_HEADS].set(a)
    a_log, dt_bias = pad_heads(gdn_a_log), pad_heads(gdn_dt_bias)
    gdn_nw = gdn_norm_w.reshape(layers, 1, HEAD_DIM)

    h = x.reshape(batch * seq, d)
    for l in range(layers):
        qkv_sb, proj, gates = _in_proj(h, norm1_w, w_in16, l, tm=1024)
        y_sb = _sb_attention(qkv_sb, batch, seq, blk=256, heads_per_step=6)
        y_gdn = _gdn_mixer(proj, gates, gdn_conv_w, a_log, dt_bias, gdn_nw, l, batch, seq, 3 * SC_WIDTH,
                           chunks_per_step=4)
        h = _out_proj(y_sb, proj, sc_conv_w, y_gdn, w_out16, l, h, seq, tm=512)
        hidden = _norm_matmul(h, norm2_w, w_up16, l, BF16, tm=1024, tn=1024, sq_relu=True)
        if l < layers - 1:
            h = _matmul_res(hidden, w_down16, l, h, final_norm_w, tm=1024, tn=1024, tk=4096, final_norm=False)
        else:
            h = _matmul_res(hidden, w_down16, l, h, final_norm_w, tm=512, tn=d, tk=2048, final_norm=True)
    return h.reshape(batch, seq, d)
```

```python
import functools

import jax
import jax.numpy as jnp
from jax import lax
from jax.experimental import pallas as pl
from jax.experimental.pallas import tpu as pltpu

HEAD_DIM = 128
SB_HEADS = 6
SB_WIDTH = SB_HEADS * HEAD_DIM
SC_WIDTH = 4 * HEAD_DIM
GDN_HEADS = 6
GDN_WIDTH = GDN_HEADS * HEAD_DIM
SC_CONV = 3
GDN_CONV = 4
GDN_CHUNK = 64
RMS_EPS = 1e-6
L2_EPS = 1e-6
HALO = 8
VMEM_LIMIT_BYTES = 56 * 1024 * 1024

F32 = jnp.float32
BF16 = jnp.bfloat16


def _cparams(*sem):
    return pltpu.CompilerParams(dimension_semantics=sem, vmem_limit_bytes=VMEM_LIMIT_BYTES)


def _dot(a, b):
    return jnp.dot(a, b, preferred_element_type=F32)


def _dot_nt(a, b):
    return lax.dot_general(a, b, (((1,), (1,)), ((), ())), preferred_element_type=F32)


def _dot_tn(a, b):
    return lax.dot_general(a, b, (((0,), (0,)), ((), ())), preferred_element_type=F32)


def _rms_normalize(x_ref, nw_ref):
    x = x_ref[...]
    r = lax.rsqrt(jnp.mean(x * x, axis=-1, keepdims=True) + RMS_EPS)
    return (x * r * nw_ref[...]).astype(BF16)


def _norm_matmul_kernel(x_ref, nw_ref, w_ref, o_ref, hn_ref, *, sq_relu):
    @pl.when(pl.program_id(1) == 0)
    def _():
        hn_ref[...] = _rms_normalize(x_ref, nw_ref)

    acc = _dot(hn_ref[...], w_ref[...])
    if sq_relu:
        acc = jnp.square(jnp.maximum(acc, 0.0))
    o_ref[...] = acc.astype(o_ref.dtype)


def _norm_matmul(x, norm_w, w, layer, out_dtype, *, tm, tn, sq_relu=False):
    m, d = x.shape
    n = w.shape[2]
    tm, tn = min(tm, m), min(tn, n)
    assert n % tn == 0 and m % tm == 0
    return pl.pallas_call(
        functools.partial(_norm_matmul_kernel, sq_relu=sq_relu),
        grid=(m // tm, n // tn),
        in_specs=[pl.BlockSpec((tm, d), lambda i, j: (i, 0)),
                  pl.BlockSpec((None, 1, d), lambda i, j: (layer, 0, 0)),
                  pl.BlockSpec((None, d, tn), lambda i, j: (layer, 0, j))],
        out_specs=pl.BlockSpec((tm, tn), lambda i, j: (i, j)),
        out_shape=jax.ShapeDtypeStruct((m, n), out_dtype),
        scratch_shapes=[pltpu.VMEM((tm, d), BF16)],
        compiler_params=_cparams("parallel", "arbitrary"),
        name="norm_matmul",
    )(x, norm_w.reshape(norm_w.shape[0], 1, d), w)


IN_TILE = 768
SB_COLS = 3 * SB_WIDTH
PROJ_COLS = 3 * SC_WIDTH + 4 * GDN_WIDTH
IN_DIM = SB_COLS + PROJ_COLS + 2 * GDN_HEADS
SB_TILES = SB_COLS // IN_TILE
PROJ_TILES = PROJ_COLS // IN_TILE


def _in_proj_kernel(x_ref, nw_ref, w_ref, sb_ref, proj_ref, gate_ref, hn_ref):
    j = pl.program_id(1)

    @pl.when(j == 0)
    def _():
        hn_ref[...] = _rms_normalize(x_ref, nw_ref)
        w = w_ref[:, :HEAD_DIM]
        lane = lax.broadcasted_iota(jnp.int32, w.shape, 1)
        gate_ref[...] = _dot(hn_ref[...], jnp.where(lane < 2 * GDN_HEADS, w, jnp.zeros_like(w)))

    @pl.when(jnp.logical_and(j >= 1, j <= SB_TILES))
    def _():
        sb_ref[...] = _dot(hn_ref[...], w_ref[...]).astype(sb_ref.dtype)

    @pl.when(j > SB_TILES)
    def _():
        proj_ref[...] = _dot(hn_ref[...], w_ref[...])


def _in_proj(x, norm_w, w_in16, layer, *, tm):
    m, d = x.shape
    tm = min(tm, m)
    assert m % tm == 0 and w_in16.shape[2] == IN_DIM
    return pl.pallas_call(
        _in_proj_kernel,
        grid=(m // tm, SB_TILES + PROJ_TILES + 1),
        in_specs=[pl.BlockSpec((tm, d), lambda i, j: (i, 0)),
                  pl.BlockSpec((None, 1, d), lambda i, j: (layer, 0, 0)),
                  pl.BlockSpec((None, d, IN_TILE),
                               lambda i, j: (layer, 0, jnp.where(j == 0, SB_TILES + PROJ_TILES, j - 1)))],
        out_specs=[pl.BlockSpec((tm, IN_TILE), lambda i, j: (i, jnp.clip(j - 1, 0, SB_TILES - 1))),
                   pl.BlockSpec((tm, IN_TILE), lambda i, j: (i, jnp.clip(j - 1 - SB_TILES, 0, PROJ_TILES - 1))),
                   pl.BlockSpec((tm, HEAD_DIM), lambda i, j: (i, 0))],
        out_shape=[jax.ShapeDtypeStruct((m, SB_COLS), BF16),
                   jax.ShapeDtypeStruct((m, PROJ_COLS), F32),
                   jax.ShapeDtypeStruct((m, HEAD_DIM), F32)],
        scratch_shapes=[pltpu.VMEM((tm, d), BF16)],
        compiler_params=_cparams("parallel", "arbitrary"),
        name="in_proj",
    )(x, norm_w.reshape(norm_w.shape[0], 1, d), w_in16)


def _softplus(z):
    return jnp.maximum(z, 0.0) + jnp.log(1.0 + jnp.exp(-jnp.abs(z)))


LOG2_E = 1.4426950408889634
SB_ZERO_MASS_LOG2 = 160.0
SB_ROW_GROUPS = 1


def _softplus_log2(z2):
    return jnp.maximum(z2, 0.0) + jnp.log(1.0 + jnp.exp2(-jnp.abs(z2))) * LOG2_E


def _attn_kernel(q_ref, k_ref, v_ref, o_ref, *, blk, scale, heads):
    i = pl.program_id(2)
    row = lax.broadcasted_iota(jnp.int32, (blk, blk), 0)
    col = lax.broadcasted_iota(jnp.int32, (blk, blk), 1)
    suffix_ones = jnp.where(row >= col, 1.0, 0.0).astype(BF16)
    causal = col < row
    lanes = lambda hd: slice(hd * HEAD_DIM, (hd + 1) * HEAD_DIM)
    step = blk // SB_ROW_GROUPS
    groups = [(hd, slice(r * step, (r + 1) * step)) for hd in range(heads) for r in range(SB_ROW_GROUPS)]
    q = [q_ref[:, lanes(hd)] for hd in range(heads)]

    def suffix_sum(sp):
        hi = sp.astype(BF16)
        lo = (sp - hi.astype(F32)).astype(BF16)
        return _dot(hi, suffix_ones) + _dot(lo, suffix_ones)

    def key_block(ref, j, hd):
        return ref[pl.ds(pl.multiple_of(j * blk, blk), blk), lanes(hd)]

    def tile(blocks, carry, acc, diagonal_last, keep_first):
        n = len(blocks)
        kb = [jnp.concatenate([key_block(k_ref, j, hd) for j in blocks], axis=0) if n > 1
              else key_block(k_ref, blocks[0], hd) for hd in range(heads)]
        vb = [[key_block(v_ref, j, hd) for j in blocks] for hd in range(heads)]
        z2 = [_dot_nt(q[hd][g], kb[hd]) * (scale * LOG2_E) for hd, g in groups]
        sp = [_softplus_log2(z) for z in z2]
        carry = [carry[hd][g] for hd, g in groups]
        acc = [acc[hd][g] for hd, g in groups]
        for part in reversed(range(n)):
            cols = slice(part * blk, (part + 1) * blk)
            on_diagonal = diagonal_last and part == n - 1
            mass, w = [], []
            for x, (hd, g) in enumerate(groups):
                sp_part = jnp.where(causal[g], sp[x][:, cols], 0.0) if on_diagonal else sp[x][:, cols]
                mass.append(suffix_sum(sp_part) + carry[x])
                carry[x] = mass[x][:, 0:1]
            for x, (hd, g) in enumerate(groups):
                wx = jnp.exp2(z2[x][:, cols] - mass[x])
                w.append((jnp.where(causal[g], wx, 0.0) if on_diagonal else wx).astype(BF16))
            for x, (hd, g) in enumerate(groups):
                contrib = _dot(w[x], vb[hd][part])
                if part == 0 and keep_first is not None:
                    contrib = jnp.where(keep_first, contrib, 0.0)
                acc[x] = acc[x] + contrib
        per_head = lambda vals: [jnp.concatenate([v for v, (h2, _) in zip(vals, groups) if h2 == hd], axis=0)
                                 for hd in range(heads)]
        return per_head(carry), per_head(acc)

    def min_mass_of(carry):
        return functools.reduce(jnp.minimum, [jnp.min(c) for c in carry])

    prev = jnp.maximum(i - 1, 0)
    carry, acc = tile([prev, i], [jnp.zeros((blk, 1), F32)] * heads, [jnp.zeros((blk, HEAD_DIM), F32)] * heads,
                      True, i > 0)
    n_left = prev

    def cond(c):
        return jnp.logical_and(c[0] < n_left // 2, c[1] <= SB_ZERO_MASS_LOG2)

    def body(c):
        j = prev - 2 * (c[0] + 1)
        carry, acc = tile([j, j + 1], c[2], c[3], False, None)
        return c[0] + 1, min_mass_of(carry), carry, acc

    _, min_mass, carry, acc = lax.while_loop(cond, body, (jnp.int32(0), min_mass_of(carry), carry, acc))

    def last(c):
        return tile([0], c[0], c[1], False, None)

    odd_left = jnp.logical_and(n_left % 2 == 1, min_mass <= SB_ZERO_MASS_LOG2)
    carry, acc = lax.cond(odd_left, last, lambda c: c, (carry, acc))
    for hd in range(heads):
        o_ref[:, lanes(hd)] = acc[hd].astype(o_ref.dtype)


def _sb_attention(qkv, batch, seq, *, blk, heads_per_step):
    assert seq % (2 * blk) == 0 and SB_HEADS % heads_per_step == 0
    nq = seq // blk
    width = heads_per_step * HEAD_DIM
    hsteps = SB_HEADS // heads_per_step
    return pl.pallas_call(
        functools.partial(_attn_kernel, blk=blk, scale=HEAD_DIM ** -0.5, heads=heads_per_step),
        grid=(batch, hsteps, nq),
        in_specs=[pl.BlockSpec((blk, width), lambda b, hh, i: (b * nq + i, hh)),
                  pl.BlockSpec((seq, width), lambda b, hh, i: (b, hsteps + hh)),
                  pl.BlockSpec((seq, width), lambda b, hh, i: (b, 2 * hsteps + hh))],
        out_specs=pl.BlockSpec((blk, width), lambda b, hh, i: (b * nq + i, hh)),
        out_shape=jax.ShapeDtypeStruct((batch * seq, SB_WIDTH), BF16),
        compiler_params=_cparams("parallel", "parallel", "arbitrary"),
        name="sb_attention",
    )(qkv, qkv, qkv)


def _halo_index(rows_per_tile):
    per = rows_per_tile // HALO
    return lambda t: jnp.maximum(t * per - 1, 0)


def _causal_conv(x_ext, w_ref, taps):
    y = w_ref[taps - 1:taps, :] * x_ext[HALO:]
    for back in range(1, taps):
        y = y + w_ref[taps - 1 - back:taps - back, :] * pltpu.roll(x_ext, back, axis=0)[HALO:]
    return y


def _gdn_kernel(q_ref, k_ref, v_ref, z_ref, qh_ref, kh_ref, vh_ref, g_ref,
                cwq_ref, cwk_ref, cwv_ref, alog_ref, dtb_ref, nw_ref,
                o_ref, state_ref, xs_ref, *, chunks):
    c = GDN_CHUNK
    rows = chunks * c
    first = pl.program_id(1) == 0

    @pl.when(first)
    def _():
        state_ref[...] = jnp.zeros_like(state_ref)

    def conv_silu(slot, x_ref, halo_ref, w_ref):
        xs_ref[slot, 0:HALO, :] = jnp.where(first, 0.0, halo_ref[...])
        xs_ref[slot, HALO:HALO + rows, :] = x_ref[...]
        y = _causal_conv(xs_ref[slot], w_ref, GDN_CONV)
        return y * jax.nn.sigmoid(y)

    q_all = conv_silu(0, q_ref, qh_ref, cwq_ref)
    k_all = conv_silu(1, k_ref, kh_ref, cwk_ref)
    v_all = conv_silu(2, v_ref, vh_ref, cwv_ref)

    gates = g_ref[...]
    g_all = -jnp.exp(alog_ref[...]) * _softplus(gates + dtb_ref[...])
    beta_all = jax.nn.sigmoid(gates)

    row = lax.broadcasted_iota(jnp.int32, (c, c), 0)
    col = lax.broadcasted_iota(jnp.int32, (c, c), 1)
    lower_ones = jnp.where(col <= row, 1.0, 0.0)
    minus_strict = jnp.where(col < row, -1.0, 0.0)
    eye = jnp.where(row == col, 1.0, 0.0)

    heads = range(GDN_HEADS)
    pairs = [(ch, h) for ch in range(chunks) for h in heads]
    rs = lambda ch: slice(ch * c, (ch + 1) * c)
    hs = lambda h: slice(h * HEAD_DIM, (h + 1) * HEAD_DIM)

    gcum_col = [_dot(lower_ones, g_all[rs(ch)]) for ch in range(chunks)]
    gcum_row = [_dot_nt(g_all[rs(ch)].T, lower_ones) for ch in range(chunks)]

    q, k, v, beta, g_c, g_last, e_g, decay = {}, {}, {}, {}, {}, {}, {}, {}
    for p in pairs:
        ch, h = p
        qq = q_all[rs(ch), hs(h)]
        kk = k_all[rs(ch), hs(h)]
        q[p] = qq * lax.rsqrt(jnp.sum(qq * qq, axis=-1, keepdims=True) + L2_EPS) * (HEAD_DIM ** -0.5)
        k[p] = kk * lax.rsqrt(jnp.sum(kk * kk, axis=-1, keepdims=True) + L2_EPS)
        v[p] = v_all[rs(ch), hs(h)]
        beta[p] = beta_all[rs(ch), GDN_HEADS + h:GDN_HEADS + h + 1]
        g_c[p] = gcum_col[ch][:, h:h + 1]
        g_r = gcum_row[ch][h:h + 1, :]
        g_last[p] = g_c[p][c - 1:c, :]
        e_g[p] = jnp.exp(g_c[p])
        decay[p] = jnp.exp(jnp.minimum(g_c[p] - g_r, 0.0)) * lower_ones

    k16 = {p: k[p].astype(BF16) for p in pairs}
    kk = {p: _dot_nt(k16[p], k16[p]) for p in pairs}
    qk = {p: _dot_nt(q[p].astype(BF16), k16[p]) for p in pairs}
    attn = {p: (qk[p] * decay[p]).astype(BF16) for p in pairs}
    pw = {p: (beta[p] * kk[p]) * (decay[p] * minus_strict) for p in pairs}
    t_mat = {p: eye + pw[p] for p in pairs}
    for _ in range(5):
        pw = {p: _dot(pw[p], pw[p]) for p in pairs}
        t_mat = {p: t_mat[p] + _dot(t_mat[p], pw[p]) for p in pairs}
    rhs = {p: jnp.concatenate([v[p] * beta[p], k[p] * (beta[p] * e_g[p])], axis=1).astype(BF16) for p in pairs}
    uw = {p: _dot(t_mat[p].astype(BF16), rhs[p]) for p in pairs}
    wq = {p: jnp.concatenate([uw[p][:, HEAD_DIM:], q[p] * e_g[p]], axis=0).astype(BF16) for p in pairs}
    k_dec = {p: (k[p] * jnp.exp(g_last[p] - g_c[p])).astype(BF16) for p in pairs}

    state = {h: state_ref[h] for h in heads}
    out = {}
    for ch in range(chunks):
        s16 = {h: state[h].astype(BF16) for h in heads}
        ws = {h: _dot(wq[ch, h], s16[h]) for h in heads}
        v_new = {h: (uw[ch, h][:, :HEAD_DIM] - ws[h][:c]).astype(BF16) for h in heads}
        out.update({(ch, h): ws[h][c:] + _dot(attn[ch, h], v_new[h]) for h in heads})
        state = {h: state[h] * jnp.exp(g_last[ch, h]) + _dot_tn(k_dec[ch, h], v_new[h]) for h in heads}
    for h in heads:
        state_ref[h] = state[h]

    for p in pairs:
        ch, h = p
        o = out[p]
        r = lax.rsqrt(jnp.mean(o * o, axis=-1, keepdims=True) + RMS_EPS)
        zz = z_ref[rs(ch), hs(h)]
        y = (o * r * nw_ref[...]) * (zz * jax.nn.sigmoid(zz))
        o_ref[rs(ch), hs(h)] = y.astype(o_ref.dtype)


def _gdn_mixer(proj, gates, conv_w, a_log, dt_bias, norm_w, layer, batch, seq, col0, *, chunks_per_step):
    rows = chunks_per_step * GDN_CHUNK
    assert seq % rows == 0
    nt = seq // rows
    cb0 = col0 // GDN_WIDTH
    hal = _halo_index(rows)
    tile = lambda cb: pl.BlockSpec((rows, GDN_WIDTH), lambda b, i: (b * nt + i, cb0 + cb))
    halo = lambda cb: pl.BlockSpec((HALO, GDN_WIDTH), lambda b, i: (hal(b * nt + i), cb0 + cb))
    cw = lambda cb: pl.BlockSpec((None, GDN_CONV, GDN_WIDTH), lambda b, i: (layer, 0, cb))
    rowvec = pl.BlockSpec((None, 1, HEAD_DIM), lambda b, i: (layer, 0, 0))
    return pl.pallas_call(
        functools.partial(_gdn_kernel, chunks=chunks_per_step),
        grid=(batch, nt),
        in_specs=[tile(0), tile(1), tile(2), tile(3), halo(0), halo(1), halo(2),
                  pl.BlockSpec((rows, HEAD_DIM), lambda b, i: (b * nt + i, 0)),
                  cw(0), cw(1), cw(2), rowvec, rowvec, rowvec],
        out_specs=pl.BlockSpec((rows, GDN_WIDTH), lambda b, i: (b * nt + i, 0)),
        out_shape=jax.ShapeDtypeStruct((batch * seq, GDN_WIDTH), BF16),
        scratch_shapes=[pltpu.VMEM((GDN_HEADS, HEAD_DIM, HEAD_DIM), F32),
                        pltpu.VMEM((3, HALO + rows, GDN_WIDTH), F32)],
        compiler_params=_cparams("parallel", "arbitrary"),
        name="gdn_mixer",
    )(proj, proj, proj, proj, proj, proj, proj, gates,
      conv_w, conv_w, conv_w, a_log, dt_bias, norm_w)


def _out_proj_kernel(ysb_ref, b_ref, c_ref, h_ref, ch_ref, hh_ref, cw_ref, ygdn_ref, w_ref, res_ref, o_ref,
                     xs_ref, *, tiles_per_seq):
    tm = b_ref.shape[0]
    first = pl.program_id(0) % tiles_per_seq == 0
    xs_ref[0:HALO, :] = jnp.where(first, 0.0, ch_ref[...] * hh_ref[...])
    xs_ref[HALO:HALO + tm, :] = c_ref[...] * h_ref[...]
    y_sc = (b_ref[...] * _causal_conv(xs_ref[...], cw_ref, SC_CONV)).astype(BF16)
    acc = res_ref[...]
    acc = acc + _dot(ysb_ref[...], w_ref[0:SB_WIDTH, :])
    acc = acc + _dot(y_sc, w_ref[SB_WIDTH:SB_WIDTH + SC_WIDTH, :])
    acc = acc + _dot(ygdn_ref[...], w_ref[SB_WIDTH + SC_WIDTH:, :])
    o_ref[...] = acc


def _out_proj(y_sb, proj, sc_conv_w, y_gdn, w, layer, res, seq, *, tm):
    m, d = res.shape
    kdim = w.shape[1]
    tm = min(tm, seq)
    assert seq % tm == 0 and m % seq == 0
    hal = _halo_index(tm)
    rows = lambda width, cb=0: pl.BlockSpec((tm, width), lambda i: (i, cb))
    halo = lambda cb: pl.BlockSpec((HALO, SC_WIDTH), lambda i: (hal(i), cb))
    return pl.pallas_call(
        functools.partial(_out_proj_kernel, tiles_per_seq=seq // tm),
        grid=(m // tm,),
        in_specs=[rows(SB_WIDTH), rows(SC_WIDTH, 0), rows(SC_WIDTH, 1), rows(SC_WIDTH, 2), halo(1), halo(2),
                  pl.BlockSpec((None, SC_CONV, SC_WIDTH), lambda i: (layer, 0, 0)),
                  rows(GDN_WIDTH),
                  pl.BlockSpec((None, kdim, d), lambda i: (layer, 0, 0)),
                  rows(d)],
        out_specs=rows(d),
        out_shape=jax.ShapeDtypeStruct((m, d), F32),
        scratch_shapes=[pltpu.VMEM((HALO + tm, SC_WIDTH), F32)],
        compiler_params=_cparams("parallel"),
        name="out_proj",
    )(y_sb, proj, proj, proj, proj, proj, sc_conv_w, y_gdn, w, res)


def _matmul_res_kernel(a_ref, w_ref, res_ref, nw_ref, o_ref, *, final_norm):
    k = pl.program_id(2)

    @pl.when(k == 0)
    def _():
        o_ref[...] = res_ref[...]

    o_ref[...] += _dot(a_ref[...], w_ref[...])

    if final_norm:
        @pl.when(k == pl.num_programs(2) - 1)
        def _():
            x = o_ref[...]
            r = lax.rsqrt(jnp.mean(x * x, axis=-1, keepdims=True) + RMS_EPS)
            o_ref[...] = x * r * nw_ref[...]


def _matmul_res(a, w, layer, res, norm_w, *, tm, tn, tk, final_norm):
    m, kdim = a.shape
    n = w.shape[2]
    tm, tn, tk = min(tm, m), min(tn, n), min(tk, kdim)
    assert m % tm == 0 and n % tn == 0 and kdim % tk == 0 and (tn == n or not final_norm)
    return pl.pallas_call(
        functools.partial(_matmul_res_kernel, final_norm=final_norm),
        grid=(m // tm, n // tn, kdim // tk),
        in_specs=[pl.BlockSpec((tm, tk), lambda i, j, k: (i, k)),
                  pl.BlockSpec((None, tk, tn), lambda i, j, k: (layer, k, j)),
                  pl.BlockSpec((tm, tn), lambda i, j, k: (i, j)),
                  pl.BlockSpec((1, tn), lambda i, j, k: (0, j))],
        out_specs=pl.BlockSpec((tm, tn), lambda i, j, k: (i, j)),
        out_shape=jax.ShapeDtypeStruct((m, n), F32),
        compiler_params=_cparams("parallel", "parallel", "arbitrary"),
        name="matmul_res",
    )(a, w, res, norm_w.reshape(1, n))


def kernel(x, norm1_w, w_in, sc_conv_w, gdn_conv_w, gdn_a_log, gdn_dt_bias, gdn_norm_w, w_out, norm2_w,
           w_up, w_down, final_norm_w):
    batch, seq, d = x.shape
    layers = norm1_w.shape[0]
    w_in16, w_out16, w_up16, w_down16 = (w.astype(BF16) for w in (w_in, w_out, w_up, w_down))
    pad_heads = lambda a: jnp.zeros((layers, 1, HEAD_DIM), F32).at[:, 0, :GDN_HEADS].set(a)
    a_log, dt_bias = pad_heads(gdn_a_log), pad_heads(gdn_dt_bias)
    gdn_nw = gdn_norm_w.reshape(layers, 1, HEAD_DIM)

    h = x.reshape(batch * seq, d)
    for l in range(layers):
        qkv_sb, proj, gates = _in_proj(h, norm1_w, w_in16, l, tm=1024)
        y_sb = _sb_attention(qkv_sb, batch, seq, blk=256, heads_per_step=6)
        y_gdn = _gdn_mixer(proj, gates, gdn_conv_w, a_log, dt_bias, gdn_nw, l, batch, seq, 3 * SC_WIDTH,
                           chunks_per_step=8)
        h = _out_proj(y_sb, proj, sc_conv_w, y_gdn, w_out16, l, h, seq, tm=512)
        hidden = _norm_matmul(h, norm2_w, w_up16, l, BF16, tm=1024, tn=1024, sq_relu=True)
        if l < layers - 1:
            h = _matmul_res(hidden, w_down16, l, h, final_norm_w, tm=1024, tn=1024, tk=4096, final_norm=False)
        else:
            h = _matmul_res(hidden, w_down16, l, h, final_norm_w, tm=512, tn=d, tk=2048, final_norm=True)
    return h.reshape(batch, seq, d)
```

```python
import functools

import jax
import jax.numpy as jnp
from jax import lax
from jax.experimental import pallas as pl
from jax.experimental.pallas import tpu as pltpu

HEAD_DIM = 128
SB_HEADS = 6
SB_WIDTH = SB_HEADS * HEAD_DIM
SC_WIDTH = 4 * HEAD_DIM
GDN_HEADS = 6
GDN_WIDTH = GDN_HEADS * HEAD_DIM
SC_CONV = 3
GDN_CONV = 4
GDN_CHUNK = 64
RMS_EPS = 1e-6
L2_EPS = 1e-6
HALO = 8
VMEM_LIMIT_BYTES = 56 * 1024 * 1024

F32 = jnp.float32
BF16 = jnp.bfloat16


def _cparams(*sem):
    return pltpu.CompilerParams(dimension_semantics=sem, vmem_limit_bytes=VMEM_LIMIT_BYTES)


def _dot(a, b):
    return jnp.dot(a, b, preferred_element_type=F32)


def _dot_nt(a, b):
    return lax.dot_general(a, b, (((1,), (1,)), ((), ())), preferred_element_type=F32)


def _dot_tn(a, b):
    return lax.dot_general(a, b, (((0,), (0,)), ((), ())), preferred_element_type=F32)


def _rms_normalize(x_ref, nw_ref):
    x = x_ref[...]
    r = lax.rsqrt(jnp.mean(x * x, axis=-1, keepdims=True) + RMS_EPS)
    return (x * r * nw_ref[...]).astype(BF16)


def _norm_matmul_kernel(x_ref, nw_ref, w_ref, o_ref, hn_ref, *, sq_relu):
    @pl.when(pl.program_id(1) == 0)
    def _():
        hn_ref[...] = _rms_normalize(x_ref, nw_ref)

    acc = _dot(hn_ref[...], w_ref[...])
    if sq_relu:
        acc = jnp.square(jnp.maximum(acc, 0.0))
    o_ref[...] = acc.astype(o_ref.dtype)


def _norm_matmul(x, norm_w, w, layer, out_dtype, *, tm, tn, sq_relu=False):
    m, d = x.shape
    n = w.shape[2]
    tm, tn = min(tm, m), min(tn, n)
    assert n % tn == 0 and m % tm == 0
    return pl.pallas_call(
        functools.partial(_norm_matmul_kernel, sq_relu=sq_relu),
        grid=(m // tm, n // tn),
        in_specs=[pl.BlockSpec((tm, d), lambda i, j: (i, 0)),
                  pl.BlockSpec((None, 1, d), lambda i, j: (layer, 0, 0)),
                  pl.BlockSpec((None, d, tn), lambda i, j: (layer, 0, j))],
        out_specs=pl.BlockSpec((tm, tn), lambda i, j: (i, j)),
        out_shape=jax.ShapeDtypeStruct((m, n), out_dtype),
        scratch_shapes=[pltpu.VMEM((tm, d), BF16)],
        compiler_params=_cparams("parallel", "arbitrary"),
        name="norm_matmul",
    )(x, norm_w.reshape(norm_w.shape[0], 1, d), w)


IN_TILE = 768
SB_COLS = 3 * SB_WIDTH
PROJ_COLS = 3 * SC_WIDTH + 4 * GDN_WIDTH
IN_DIM = SB_COLS + PROJ_COLS + 2 * GDN_HEADS
SB_TILES = SB_COLS // IN_TILE
PROJ_TILES = PROJ_COLS // IN_TILE


def _in_proj_kernel(x_ref, nw_ref, w_ref, wg_ref, sb_ref, proj_ref, gate_ref, hn_ref):
    j = pl.program_id(1)

    @pl.when(j == 0)
    def _():
        hn_ref[...] = _rms_normalize(x_ref, nw_ref)
        gate_ref[...] = _dot(hn_ref[...], wg_ref[...])

    @pl.when(jnp.logical_and(j >= 1, j <= SB_TILES))
    def _():
        sb_ref[...] = _dot(hn_ref[...], w_ref[...]).astype(sb_ref.dtype)

    @pl.when(j > SB_TILES)
    def _():
        proj_ref[...] = _dot(hn_ref[...], w_ref[...])


def _in_proj(x, norm_w, w_in16, w_gate16, layer, *, tm):
    m, d = x.shape
    tm = min(tm, m)
    assert m % tm == 0 and w_in16.shape[2] == IN_DIM
    return pl.pallas_call(
        _in_proj_kernel,
        grid=(m // tm, SB_TILES + PROJ_TILES + 1),
        in_specs=[pl.BlockSpec((tm, d), lambda i, j: (i, 0)),
                  pl.BlockSpec((None, 1, d), lambda i, j: (layer, 0, 0)),
                  pl.BlockSpec((None, d, IN_TILE), lambda i, j: (layer, 0, jnp.maximum(j - 1, 0))),
                  pl.BlockSpec((None, d, HEAD_DIM), lambda i, j: (layer, 0, 0))],
        out_specs=[pl.BlockSpec((tm, IN_TILE), lambda i, j: (i, jnp.clip(j - 1, 0, SB_TILES - 1))),
                   pl.BlockSpec((tm, IN_TILE), lambda i, j: (i, jnp.clip(j - 1 - SB_TILES, 0, PROJ_TILES - 1))),
                   pl.BlockSpec((tm, HEAD_DIM), lambda i, j: (i, 0))],
        out_shape=[jax.ShapeDtypeStruct((m, SB_COLS), BF16),
                   jax.ShapeDtypeStruct((m, PROJ_COLS), F32),
                   jax.ShapeDtypeStruct((m, HEAD_DIM), F32)],
        scratch_shapes=[pltpu.VMEM((tm, d), BF16)],
        compiler_params=_cparams("parallel", "arbitrary"),
        name="in_proj",
    )(x, norm_w.reshape(norm_w.shape[0], 1, d), w_in16, w_gate16)


def _softplus(z):
    return jnp.maximum(z, 0.0) + jnp.log(1.0 + jnp.exp(-jnp.abs(z)))


LOG2_E = 1.4426950408889634
SB_ZERO_MASS_LOG2 = 160.0
SB_ROW_GROUPS = 1


def _softplus_log2(z2):
    return jnp.maximum(z2, 0.0) + jnp.log(1.0 + jnp.exp2(-jnp.abs(z2))) * LOG2_E


def _attn_kernel(q_ref, k_ref, v_ref, o_ref, *, blk, scale, heads):
    i = pl.program_id(2)
    row = lax.broadcasted_iota(jnp.int32, (blk, blk), 0)
    col = lax.broadcasted_iota(jnp.int32, (blk, blk), 1)
    suffix_ones = jnp.where(row >= col, 1.0, 0.0).astype(BF16)
    causal = col < row
    lanes = lambda hd: slice(hd * HEAD_DIM, (hd + 1) * HEAD_DIM)
    step = blk // SB_ROW_GROUPS
    groups = [(hd, slice(r * step, (r + 1) * step)) for hd in range(heads) for r in range(SB_ROW_GROUPS)]
    q = [q_ref[:, lanes(hd)] for hd in range(heads)]

    def suffix_sum(sp):
        hi = sp.astype(BF16)
        lo = (sp - hi.astype(F32)).astype(BF16)
        return _dot(hi, suffix_ones) + _dot(lo, suffix_ones)

    def key_block(ref, j, hd):
        return ref[pl.ds(pl.multiple_of(j * blk, blk), blk), lanes(hd)]

    def tile(blocks, carry, acc, diagonal_last, keep_first):
        n = len(blocks)
        kb = [jnp.concatenate([key_block(k_ref, j, hd) for j in blocks], axis=0) if n > 1
              else key_block(k_ref, blocks[0], hd) for hd in range(heads)]
        vb = [[key_block(v_ref, j, hd) for j in blocks] for hd in range(heads)]
        z2 = [_dot_nt(q[hd][g], kb[hd]) * (scale * LOG2_E) for hd, g in groups]
        sp = [_softplus_log2(z) for z in z2]
        carry = [carry[hd][g] for hd, g in groups]
        acc = [acc[hd][g] for hd, g in groups]
        for part in reversed(range(n)):
            cols = slice(part * blk, (part + 1) * blk)
            on_diagonal = diagonal_last and part == n - 1
            mass, w = [], []
            for x, (hd, g) in enumerate(groups):
                sp_part = jnp.where(causal[g], sp[x][:, cols], 0.0) if on_diagonal else sp[x][:, cols]
                mass.append(suffix_sum(sp_part) + carry[x])
                carry[x] = mass[x][:, 0:1]
            for x, (hd, g) in enumerate(groups):
                wx = jnp.exp2(z2[x][:, cols] - mass[x])
                w.append((jnp.where(causal[g], wx, 0.0) if on_diagonal else wx).astype(BF16))
            for x, (hd, g) in enumerate(groups):
                contrib = _dot(w[x], vb[hd][part])
                if part == 0 and keep_first is not None:
                    contrib = jnp.where(keep_first, contrib, 0.0)
                acc[x] = acc[x] + contrib
        per_head = lambda vals: [jnp.concatenate([v for v, (h2, _) in zip(vals, groups) if h2 == hd], axis=0)
                                 for hd in range(heads)]
        return per_head(carry), per_head(acc)

    def min_mass_of(carry):
        return functools.reduce(jnp.minimum, [jnp.min(c) for c in carry])

    prev = jnp.maximum(i - 1, 0)
    carry, acc = tile([prev, i], [jnp.zeros((blk, 1), F32)] * heads, [jnp.zeros((blk, HEAD_DIM), F32)] * heads,
                      True, i > 0)
    n_left = prev

    def cond(c):
        return jnp.logical_and(c[0] < n_left // 2, c[1] <= SB_ZERO_MASS_LOG2)

    def body(c):
        j = prev - 2 * (c[0] + 1)
        carry, acc = tile([j, j + 1], c[2], c[3], False, None)
        return c[0] + 1, min_mass_of(carry), carry, acc

    _, min_mass, carry, acc = lax.while_loop(cond, body, (jnp.int32(0), min_mass_of(carry), carry, acc))

    def last(c):
        return tile([0], c[0], c[1], False, None)

    odd_left = jnp.logical_and(n_left % 2 == 1, min_mass <= SB_ZERO_MASS_LOG2)
    carry, acc = lax.cond(odd_left, last, lambda c: c, (carry, acc))
    for hd in range(heads):
        o_ref[:, lanes(hd)] = acc[hd].astype(o_ref.dtype)


def _sb_attention(qkv, batch, seq, *, blk, heads_per_step):
    assert seq % (2 * blk) == 0 and SB_HEADS % heads_per_step == 0
    nq = seq // blk
    width = heads_per_step * HEAD_DIM
    hsteps = SB_HEADS // heads_per_step
    return pl.pallas_call(
        functools.partial(_attn_kernel, blk=blk, scale=HEAD_DIM ** -0.5, heads=heads_per_step),
        grid=(batch, hsteps, nq),
        in_specs=[pl.BlockSpec((blk, width), lambda b, hh, i: (b * nq + i, hh)),
                  pl.BlockSpec((seq, width), lambda b, hh, i: (b, hsteps + hh)),
                  pl.BlockSpec((seq, width), lambda b, hh, i: (b, 2 * hsteps + hh))],
        out_specs=pl.BlockSpec((blk, width), lambda b, hh, i: (b * nq + i, hh)),
        out_shape=jax.ShapeDtypeStruct((batch * seq, SB_WIDTH), BF16),
        compiler_params=_cparams("parallel", "parallel", "arbitrary"),
        name="sb_attention",
    )(qkv, qkv, qkv)


def _halo_index(rows_per_tile):
    per = rows_per_tile // HALO
    return lambda t: jnp.maximum(t * per - 1, 0)


def _causal_conv(x_ext, w_ref, taps):
    y = w_ref[taps - 1:taps, :] * x_ext[HALO:]
    for back in range(1, taps):
        y = y + w_ref[taps - 1 - back:taps - back, :] * pltpu.roll(x_ext, back, axis=0)[HALO:]
    return y


def _gdn_kernel(q_ref, k_ref, v_ref, z_ref, qh_ref, kh_ref, vh_ref, g_ref,
                cwq_ref, cwk_ref, cwv_ref, alog_ref, dtb_ref, nw_ref,
                o_ref, state_ref, xs_ref, *, chunks):
    c = GDN_CHUNK
    rows = chunks * c
    first = pl.program_id(1) == 0

    @pl.when(first)
    def _():
        state_ref[...] = jnp.zeros_like(state_ref)

    def conv_silu(slot, x_ref, halo_ref, w_ref):
        xs_ref[slot, 0:HALO, :] = jnp.where(first, 0.0, halo_ref[...])
        xs_ref[slot, HALO:HALO + rows, :] = x_ref[...]
        y = _causal_conv(xs_ref[slot], w_ref, GDN_CONV)
        return y * jax.nn.sigmoid(y)

    q_all = conv_silu(0, q_ref, qh_ref, cwq_ref)
    k_all = conv_silu(1, k_ref, kh_ref, cwk_ref)
    v_all = conv_silu(2, v_ref, vh_ref, cwv_ref)

    gates = g_ref[...]
    g_all = -jnp.exp(alog_ref[...]) * _softplus(gates + dtb_ref[...])
    beta_all = jax.nn.sigmoid(gates)

    row = lax.broadcasted_iota(jnp.int32, (c, c), 0)
    col = lax.broadcasted_iota(jnp.int32, (c, c), 1)
    lower_ones = jnp.where(col <= row, 1.0, 0.0)
    minus_strict = jnp.where(col < row, -1.0, 0.0)
    eye = jnp.where(row == col, 1.0, 0.0)

    heads = range(GDN_HEADS)
    pairs = [(ch, h) for ch in range(chunks) for h in heads]
    rs = lambda ch: slice(ch * c, (ch + 1) * c)
    hs = lambda h: slice(h * HEAD_DIM, (h + 1) * HEAD_DIM)

    gcum_col = [_dot(lower_ones, g_all[rs(ch)]) for ch in range(chunks)]
    gcum_row = [_dot_nt(g_all[rs(ch)].T, lower_ones) for ch in range(chunks)]

    q, k, v, beta, g_c, g_last, e_g, decay = {}, {}, {}, {}, {}, {}, {}, {}
    for p in pairs:
        ch, h = p
        qq = q_all[rs(ch), hs(h)]
        kk = k_all[rs(ch), hs(h)]
        q[p] = qq * lax.rsqrt(jnp.sum(qq * qq, axis=-1, keepdims=True) + L2_EPS) * (HEAD_DIM ** -0.5)
        k[p] = kk * lax.rsqrt(jnp.sum(kk * kk, axis=-1, keepdims=True) + L2_EPS)
        v[p] = v_all[rs(ch), hs(h)]
        beta[p] = beta_all[rs(ch), GDN_HEADS + h:GDN_HEADS + h + 1]
        g_c[p] = gcum_col[ch][:, h:h + 1]
        g_r = gcum_row[ch][h:h + 1, :]
        g_last[p] = g_c[p][c - 1:c, :]
        e_g[p] = jnp.exp(g_c[p])
        decay[p] = jnp.exp(jnp.minimum(g_c[p] - g_r, 0.0)) * lower_ones

    k16 = {p: k[p].astype(BF16) for p in pairs}
    kk = {p: _dot_nt(k16[p], k16[p]) for p in pairs}
    qk = {p: _dot_nt(q[p].astype(BF16), k16[p]) for p in pairs}
    attn = {p: (qk[p] * decay[p]).astype(BF16) for p in pairs}
    pw = {p: (beta[p] * kk[p]) * (decay[p] * minus_strict) for p in pairs}
    t_mat = {p: eye + pw[p] for p in pairs}
    for _ in range(5):
        pw = {p: _dot(pw[p], pw[p]) for p in pairs}
        t_mat = {p: t_mat[p] + _dot(t_mat[p], pw[p]) for p in pairs}
    rhs = {p: jnp.concatenate([v[p] * beta[p], k[p] * (beta[p] * e_g[p])], axis=1).astype(BF16) for p in pairs}
    uw = {p: _dot(t_mat[p].astype(BF16), rhs[p]) for p in pairs}
    wq = {p: jnp.concatenate([uw[p][:, HEAD_DIM:], q[p] * e_g[p]], axis=0).astype(BF16) for p in pairs}
    k_dec = {p: (k[p] * jnp.exp(g_last[p] - g_c[p])).astype(BF16) for p in pairs}

    state = {h: state_ref[h] for h in heads}
    out = {}
    for ch in range(chunks):
        s16 = {h: state[h].astype(BF16) for h in heads}
        ws = {h: _dot(wq[ch, h], s16[h]) for h in heads}
        v_new = {h: (uw[ch, h][:, :HEAD_DIM] - ws[h][:c]).astype(BF16) for h in heads}
        out.update({(ch, h): ws[h][c:] + _dot(attn[ch, h], v_new[h]) for h in heads})
        state = {h: state[h] * jnp.exp(g_last[ch, h]) + _dot_tn(k_dec[ch, h], v_new[h]) for h in heads}
    for h in heads:
        state_ref[h] = state[h]

    for p in pairs:
        ch, h = p
        o = out[p]
        r = lax.rsqrt(jnp.mean(o * o, axis=-1, keepdims=True) + RMS_EPS)
        zz = z_ref[rs(ch), hs(h)]
        y = (o * r * nw_ref[...]) * (zz * jax.nn.sigmoid(zz))
        o_ref[rs(ch), hs(h)] = y.astype(o_ref.dtype)


def _gdn_mixer(proj, gates, conv_w, a_log, dt_bias, norm_w, layer, batch, seq, col0, *, chunks_per_step):
    rows = chunks_per_step * GDN_CHUNK
    assert seq % rows == 0
    nt = seq // rows
    cb0 = col0 // GDN_WIDTH
    hal = _halo_index(rows)
    tile = lambda cb: pl.BlockSpec((rows, GDN_WIDTH), lambda b, i: (b * nt + i, cb0 + cb))
    halo = lambda cb: pl.BlockSpec((HALO, GDN_WIDTH), lambda b, i: (hal(b * nt + i), cb0 + cb))
    cw = lambda cb: pl.BlockSpec((None, GDN_CONV, GDN_WIDTH), lambda b, i: (layer, 0, cb))
    rowvec = pl.BlockSpec((None, 1, HEAD_DIM), lambda b, i: (layer, 0, 0))
    return pl.pallas_call(
        functools.partial(_gdn_kernel, chunks=chunks_per_step),
        grid=(batch, nt),
        in_specs=[tile(0), tile(1), tile(2), tile(3), halo(0), halo(1), halo(2),
                  pl.BlockSpec((rows, HEAD_DIM), lambda b, i: (b * nt + i, 0)),
                  cw(0), cw(1), cw(2), rowvec, rowvec, rowvec],
        out_specs=pl.BlockSpec((rows, GDN_WIDTH), lambda b, i: (b * nt + i, 0)),
        out_shape=jax.ShapeDtypeStruct((batch * seq, GDN_WIDTH), BF16),
        scratch_shapes=[pltpu.VMEM((GDN_HEADS, HEAD_DIM, HEAD_DIM), F32),
                        pltpu.VMEM((3, HALO + rows, GDN_WIDTH), F32)],
        compiler_params=_cparams("parallel", "arbitrary"),
        name="gdn_mixer",
    )(proj, proj, proj, proj, proj, proj, proj, gates,
      conv_w, conv_w, conv_w, a_log, dt_bias, norm_w)


def _out_proj_kernel(ysb_ref, b_ref, c_ref, h_ref, ch_ref, hh_ref, cw_ref, ygdn_ref, w_ref, res_ref, o_ref,
                     xs_ref, *, tiles_per_seq):
    tm = b_ref.shape[0]
    first = pl.program_id(0) % tiles_per_seq == 0
    xs_ref[0:HALO, :] = jnp.where(first, 0.0, ch_ref[...] * hh_ref[...])
    xs_ref[HALO:HALO + tm, :] = c_ref[...] * h_ref[...]
    y_sc = (b_ref[...] * _causal_conv(xs_ref[...], cw_ref, SC_CONV)).astype(BF16)
    acc = res_ref[...]
    acc = acc + _dot(ysb_ref[...], w_ref[0:SB_WIDTH, :])
    acc = acc + _dot(y_sc, w_ref[SB_WIDTH:SB_WIDTH + SC_WIDTH, :])
    acc = acc + _dot(ygdn_ref[...], w_ref[SB_WIDTH + SC_WIDTH:, :])
    o_ref[...] = acc


def _out_proj(y_sb, proj, sc_conv_w, y_gdn, w, layer, res, seq, *, tm):
    m, d = res.shape
    kdim = w.shape[1]
    tm = min(tm, seq)
    assert seq % tm == 0 and m % seq == 0
    hal = _halo_index(tm)
    rows = lambda width, cb=0: pl.BlockSpec((tm, width), lambda i: (i, cb))
    halo = lambda cb: pl.BlockSpec((HALO, SC_WIDTH), lambda i: (hal(i), cb))
    return pl.pallas_call(
        functools.partial(_out_proj_kernel, tiles_per_seq=seq // tm),
        grid=(m // tm,),
        in_specs=[rows(SB_WIDTH), rows(SC_WIDTH, 0), rows(SC_WIDTH, 1), rows(SC_WIDTH, 2), halo(1), halo(2),
                  pl.BlockSpec((None, SC_CONV, SC_WIDTH), lambda i: (layer, 0, 0)),
                  rows(GDN_WIDTH),
                  pl.BlockSpec((None, kdim, d), lambda i: (layer, 0, 0)),
                  rows(d)],
        out_specs=rows(d),
        out_shape=jax.ShapeDtypeStruct((m, d), F32),
        scratch_shapes=[pltpu.VMEM((HALO + tm, SC_WIDTH), F32)],
        compiler_params=_cparams("parallel"),
        name="out_proj",
    )(y_sb, proj, proj, proj, proj, proj, sc_conv_w, y_gdn, w, res)


def _matmul_res_kernel(a_ref, w_ref, res_ref, nw_ref, o_ref, *, final_norm):
    k = pl.program_id(2)

    @pl.when(k == 0)
    def _():
        o_ref[...] = res_ref[...]

    o_ref[...] += _dot(a_ref[...], w_ref[...])

    if final_norm:
        @pl.when(k == pl.num_programs(2) - 1)
        def _():
            x = o_ref[...]
            r = lax.rsqrt(jnp.mean(x * x, axis=-1, keepdims=True) + RMS_EPS)
            o_ref[...] = x * r * nw_ref[...]


def _matmul_res(a, w, layer, res, norm_w, *, tm, tn, tk, final_norm):
    m, kdim = a.shape
    n = w.shape[2]
    tm, tn, tk = min(tm, m), min(tn, n), min(tk, kdim)
    assert m % tm == 0 and n % tn == 0 and kdim % tk == 0 and (tn == n or not final_norm)
    return pl.pallas_call(
        functools.partial(_matmul_res_kernel, final_norm=final_norm),
        grid=(m // tm, n // tn, kdim // tk),
        in_specs=[pl.BlockSpec((tm, tk), lambda i, j, k: (i, k)),
                  pl.BlockSpec((None, tk, tn), lambda i, j, k: (layer, k, j)),
                  pl.BlockSpec((tm, tn), lambda i, j, k: (i, j)),
                  pl.BlockSpec((1, tn), lambda i, j, k: (0, j))],
        out_specs=pl.BlockSpec((tm, tn), lambda i, j, k: (i, j)),
        out_shape=jax.ShapeDtypeStruct((m, n), F32),
        compiler_params=_cparams("parallel", "parallel", "arbitrary"),
        name="matmul_res",
    )(a, w, res, norm_w.reshape(1, n))


def kernel(x, norm1_w, w_in, sc_conv_w, gdn_conv_w, gdn_a_log, gdn_dt_bias, gdn_norm_w, w_out, norm2_w,
           w_up, w_down, final_norm_w):
    batch, seq, d = x.shape
    layers = norm1_w.shape[0]
    w_in16, w_out16, w_up16, w_down16 = (w.astype(BF16) for w in (w_in, w_out, w_up, w_down))
    w_gate16 = jnp.zeros((layers, d, HEAD_DIM), BF16).at[:, :, :2 * GDN_HEADS].set(
        w_in[:, :, SB_COLS + PROJ_COLS:].astype(BF16))
    pad_heads = lambda a: jnp.zeros((layers, 1, HEAD_DIM), F32).at[:, 0, :GDN_HEADS].set(a)
    a_log, dt_bias = pad_heads(gdn_a_log), pad_heads(gdn_dt_bias)
    gdn_nw = gdn_norm_w.reshape(layers, 1, HEAD_DIM)

    h = x.reshape(batch * seq, d)
    for l in range(layers):
        qkv_sb, proj, gates = _in_proj(h, norm1_w, w_in16, w_gate16, l, tm=1024)
        y_sb = _sb_attention(qkv_sb, batch, seq, blk=256, heads_per_step=6)
        y_gdn = _gdn_mixer(proj, gates, gdn_conv_w, a_log, dt_bias, gdn_nw, l, batch, seq, 3 * SC_WIDTH,
                           chunks_per_step=8)
        h = _out_proj(y_sb, proj, sc_conv_w, y_gdn, w_out16, l, h, seq, tm=512)
        hidden = _norm_matmul(h, norm2_w, w_up16, l, BF16, tm=1024, tn=1024, sq_relu=True)
        if l < layers - 1:
            h = _matmul_res(hidden, w_down16, l, h, final_norm_w, tm=1024, tn=1024, tk=4096, final_norm=False)
        else:
            h = _matmul_res(hidden, w_down16, l, h, final_norm_w, tm=512, tn=d, tk=2048, final_norm=True)
    return h.reshape(batch, seq, d)
```

```python
import functools

import jax
import jax.numpy as jnp
from jax import lax
from jax.experimental import pallas as pl
from jax.experimental.pallas import tpu as pltpu

HEAD_DIM = 128
SB_HEADS = 6
SB_WIDTH = SB_HEADS * HEAD_DIM
SC_WIDTH = 4 * HEAD_DIM
GDN_HEADS = 6
GDN_WIDTH = GDN_HEADS * HEAD_DIM
SC_CONV = 3
GDN_CONV = 4
GDN_CHUNK = 64
RMS_EPS = 1e-6
L2_EPS = 1e-6
HALO = 8
VMEM_LIMIT_BYTES = 56 * 1024 * 1024

F32 = jnp.float32
BF16 = jnp.bfloat16


def _cparams(*sem):
    return pltpu.CompilerParams(dimension_semantics=sem, vmem_limit_bytes=VMEM_LIMIT_BYTES)


def _dot(a, b):
    return jnp.dot(a, b, preferred_element_type=F32)


def _dot_nt(a, b):
    return lax.dot_general(a, b, (((1,), (1,)), ((), ())), preferred_element_type=F32)


def _dot_tn(a, b):
    return lax.dot_general(a, b, (((0,), (0,)), ((), ())), preferred_element_type=F32)


def _rms_normalize(x_refs, nw_ref):
    xs = [r[...] for r in x_refs]
    d = sum(x.shape[1] for x in xs)
    sumsq = functools.reduce(jnp.add, [jnp.sum(x * x, axis=-1, keepdims=True) for x in xs])
    r = lax.rsqrt(sumsq * (1.0 / d) + RMS_EPS)
    out, col = [], 0
    for x in xs:
        out.append((x * r * nw_ref[:, col:col + x.shape[1]]).astype(BF16))
        col += x.shape[1]
    return jnp.concatenate(out, axis=1)


def _row_tile_specs(tm, d, row_tiles, col_steps):
    half = d // 2
    ahead = lambda i, j: (jnp.minimum(jnp.where(j == col_steps - 1, i + 1, i), row_tiles - 1), 1)
    return [pl.BlockSpec((tm, half), lambda i, j: (i, 0)), pl.BlockSpec((tm, half), ahead)]


def _norm_matmul_kernel(x0_ref, x1_ref, nw_ref, w_ref, o_ref, hn_ref, *, sq_relu):
    @pl.when(pl.program_id(1) == 0)
    def _():
        hn_ref[...] = _rms_normalize((x0_ref, x1_ref), nw_ref)

    acc = _dot(hn_ref[...], w_ref[...])
    if sq_relu:
        acc = jnp.square(jnp.maximum(acc, 0.0))
    o_ref[...] = acc.astype(o_ref.dtype)


def _norm_matmul(x, norm_w, w, layer, out_dtype, *, tm, tn, sq_relu=False):
    m, d = x.shape
    n = w.shape[2]
    tm, tn = min(tm, m), min(tn, n)
    assert n % tn == 0 and m % tm == 0
    return pl.pallas_call(
        functools.partial(_norm_matmul_kernel, sq_relu=sq_relu),
        grid=(m // tm, n // tn),
        in_specs=_row_tile_specs(tm, d, m // tm, n // tn) + [
            pl.BlockSpec((None, 1, d), lambda i, j: (layer, 0, 0)),
            pl.BlockSpec((None, d, tn), lambda i, j: (layer, 0, j))],
        out_specs=pl.BlockSpec((tm, tn), lambda i, j: (i, j)),
        out_shape=jax.ShapeDtypeStruct((m, n), out_dtype),
        scratch_shapes=[pltpu.VMEM((tm, d), BF16)],
        compiler_params=_cparams("parallel", "arbitrary"),
        name="norm_matmul",
    )(x, x, norm_w.reshape(norm_w.shape[0], 1, d), w)


IN_TILE = 768
SB_COLS = 3 * SB_WIDTH
PROJ_COLS = 3 * SC_WIDTH + 4 * GDN_WIDTH
IN_DIM = SB_COLS + PROJ_COLS + 2 * GDN_HEADS
SB_TILES = SB_COLS // IN_TILE
PROJ_TILES = PROJ_COLS // IN_TILE


def _in_proj_kernel(x0_ref, x1_ref, nw_ref, w_ref, sb_ref, proj_ref, gate_ref, hn_ref):
    j = pl.program_id(1)

    @pl.when(j == 0)
    def _():
        hn_ref[...] = _rms_normalize((x0_ref, x1_ref), nw_ref)
        w = w_ref[:, :HEAD_DIM]
        lane = lax.broadcasted_iota(jnp.int32, w.shape, 1)
        gate_ref[...] = _dot(hn_ref[...], jnp.where(lane < 2 * GDN_HEADS, w, jnp.zeros_like(w)))

    @pl.when(jnp.logical_and(j >= 1, j <= SB_TILES))
    def _():
        sb_ref[...] = _dot(hn_ref[...], w_ref[...]).astype(sb_ref.dtype)

    @pl.when(j > SB_TILES)
    def _():
        proj_ref[...] = _dot(hn_ref[...], w_ref[...])


def _in_proj(x, norm_w, w_in16, layer, *, tm):
    m, d = x.shape
    tm = min(tm, m)
    assert m % tm == 0 and w_in16.shape[2] == IN_DIM
    return pl.pallas_call(
        _in_proj_kernel,
        grid=(m // tm, SB_TILES + PROJ_TILES + 1),
        in_specs=_row_tile_specs(tm, d, m // tm, SB_TILES + PROJ_TILES + 1) + [
            pl.BlockSpec((None, 1, d), lambda i, j: (layer, 0, 0)),
            pl.BlockSpec((None, d, IN_TILE),
                         lambda i, j: (layer, 0, jnp.where(j == 0, SB_TILES + PROJ_TILES, j - 1)))],
        out_specs=[pl.BlockSpec((tm, IN_TILE), lambda i, j: (i, jnp.clip(j - 1, 0, SB_TILES - 1))),
                   pl.BlockSpec((tm, IN_TILE), lambda i, j: (i, jnp.clip(j - 1 - SB_TILES, 0, PROJ_TILES - 1))),
                   pl.BlockSpec((tm, HEAD_DIM), lambda i, j: (i, 0))],
        out_shape=[jax.ShapeDtypeStruct((m, SB_COLS), BF16),
                   jax.ShapeDtypeStruct((m, PROJ_COLS), F32),
                   jax.ShapeDtypeStruct((m, HEAD_DIM), F32)],
        scratch_shapes=[pltpu.VMEM((tm, d), BF16)],
        compiler_params=_cparams("parallel", "arbitrary"),
        name="in_proj",
    )(x, x, norm_w.reshape(norm_w.shape[0], 1, d), w_in16)


def _softplus(z):
    return jnp.maximum(z, 0.0) + jnp.log(1.0 + jnp.exp(-jnp.abs(z)))


LOG2_E = 1.4426950408889634
SB_ZERO_MASS_LOG2 = 160.0
SB_ROW_GROUPS = 1


def _softplus_log2(z2):
    return jnp.maximum(z2, 0.0) + jnp.log(1.0 + jnp.exp2(-jnp.abs(z2))) * LOG2_E


def _attn_kernel(q_ref, k_ref, v_ref, o_ref, *, blk, scale, heads):
    i = pl.program_id(2)
    row = lax.broadcasted_iota(jnp.int32, (blk, blk), 0)
    col = lax.broadcasted_iota(jnp.int32, (blk, blk), 1)
    suffix_ones = jnp.where(row >= col, 1.0, 0.0).astype(BF16)
    causal = col < row
    lanes = lambda hd: slice(hd * HEAD_DIM, (hd + 1) * HEAD_DIM)
    step = blk // SB_ROW_GROUPS
    groups = [(hd, slice(r * step, (r + 1) * step)) for hd in range(heads) for r in range(SB_ROW_GROUPS)]
    q = [q_ref[:, lanes(hd)] for hd in range(heads)]

    def suffix_sum(sp):
        hi = sp.astype(BF16)
        lo = (sp - hi.astype(F32)).astype(BF16)
        return _dot(hi, suffix_ones) + _dot(lo, suffix_ones)

    def key_block(ref, j, hd):
        return ref[pl.ds(pl.multiple_of(j * blk, blk), blk), lanes(hd)]

    def tile(blocks, carry, acc, diagonal_last, keep_first):
        n = len(blocks)
        kb = [jnp.concatenate([key_block(k_ref, j, hd) for j in blocks], axis=0) if n > 1
              else key_block(k_ref, blocks[0], hd) for hd in range(heads)]
        vb = [[key_block(v_ref, j, hd) for j in blocks] for hd in range(heads)]
        z2 = [_dot_nt(q[hd][g], kb[hd]) * (scale * LOG2_E) for hd, g in groups]
        sp = [_softplus_log2(z) for z in z2]
        carry = [carry[hd][g] for hd, g in groups]
        acc = [acc[hd][g] for hd, g in groups]
        for part in reversed(range(n)):
            cols = slice(part * blk, (part + 1) * blk)
            on_diagonal = diagonal_last and part == n - 1
            mass, w = [], []
            for x, (hd, g) in enumerate(groups):
                sp_part = jnp.where(causal[g], sp[x][:, cols], 0.0) if on_diagonal else sp[x][:, cols]
                mass.append(suffix_sum(sp_part) + carry[x])
                carry[x] = mass[x][:, 0:1]
            for x, (hd, g) in enumerate(groups):
                wx = jnp.exp2(z2[x][:, cols] - mass[x])
                w.append((jnp.where(causal[g], wx, 0.0) if on_diagonal else wx).astype(BF16))
            for x, (hd, g) in enumerate(groups):
                contrib = _dot(w[x], vb[hd][part])
                if part == 0 and keep_first is not None:
                    contrib = jnp.where(keep_first, contrib, 0.0)
                acc[x] = acc[x] + contrib
        per_head = lambda vals: [jnp.concatenate([v for v, (h2, _) in zip(vals, groups) if h2 == hd], axis=0)
                                 for hd in range(heads)]
        return per_head(carry), per_head(acc)

    def min_mass_of(carry):
        return functools.reduce(jnp.minimum, [jnp.min(c) for c in carry])

    prev = jnp.maximum(i - 1, 0)
    carry, acc = tile([prev, i], [jnp.zeros((blk, 1), F32)] * heads, [jnp.zeros((blk, HEAD_DIM), F32)] * heads,
                      True, i > 0)
    n_left = prev

    def cond(c):
        return jnp.logical_and(c[0] < n_left // 2, c[1] <= SB_ZERO_MASS_LOG2)

    def body(c):
        j = prev - 2 * (c[0] + 1)
        carry, acc = tile([j, j + 1], c[2], c[3], False, None)
        return c[0] + 1, min_mass_of(carry), carry, acc

    _, min_mass, carry, acc = lax.while_loop(cond, body, (jnp.int32(0), min_mass_of(carry), carry, acc))

    def last(c):
        return tile([0], c[0], c[1], False, None)

    odd_left = jnp.logical_and(n_left % 2 == 1, min_mass <= SB_ZERO_MASS_LOG2)
    carry, acc = lax.cond(odd_left, last, lambda c: c, (carry, acc))
    for hd in range(heads):
        o_ref[:, lanes(hd)] = acc[hd].astype(o_ref.dtype)


def _sb_attention(qkv, batch, seq, *, blk, heads_per_step):
    assert seq % (2 * blk) == 0 and SB_HEADS % heads_per_step == 0
    nq = seq // blk
    width = heads_per_step * HEAD_DIM
    hsteps = SB_HEADS // heads_per_step
    return pl.pallas_call(
        functools.partial(_attn_kernel, blk=blk, scale=HEAD_DIM ** -0.5, heads=heads_per_step),
        grid=(batch, hsteps, nq),
        in_specs=[pl.BlockSpec((blk, width), lambda b, hh, i: (b * nq + i, hh)),
                  pl.BlockSpec((seq, width), lambda b, hh, i: (b, hsteps + hh)),
                  pl.BlockSpec((seq, width), lambda b, hh, i: (b, 2 * hsteps + hh))],
        out_specs=pl.BlockSpec((blk, width), lambda b, hh, i: (b * nq + i, hh)),
        out_shape=jax.ShapeDtypeStruct((batch * seq, SB_WIDTH), BF16),
        compiler_params=_cparams("parallel", "parallel", "arbitrary"),
        name="sb_attention",
    )(qkv, qkv, qkv)


def _halo_index(rows_per_tile):
    per = rows_per_tile // HALO
    return lambda t: jnp.maximum(t * per - 1, 0)


def _causal_conv(x_ext, w_ref, taps):
    y = w_ref[taps - 1:taps, :] * x_ext[HALO:]
    for back in range(1, taps):
        y = y + w_ref[taps - 1 - back:taps - back, :] * pltpu.roll(x_ext, back, axis=0)[HALO:]
    return y


def _gdn_kernel(q_ref, k_ref, v_ref, z_ref, qh_ref, kh_ref, vh_ref, g_ref,
                cwq_ref, cwk_ref, cwv_ref, alog_ref, dtb_ref, nw_ref,
                o_ref, state_ref, xs_ref, *, chunks):
    c = GDN_CHUNK
    rows = chunks * c
    first = pl.program_id(1) == 0

    @pl.when(first)
    def _():
        state_ref[...] = jnp.zeros_like(state_ref)

    def conv_silu(slot, x_ref, halo_ref, w_ref):
        xs_ref[slot, 0:HALO, :] = jnp.where(first, 0.0, halo_ref[...])
        xs_ref[slot, HALO:HALO + rows, :] = x_ref[...]
        y = _causal_conv(xs_ref[slot], w_ref, GDN_CONV)
        return y * jax.nn.sigmoid(y)

    q_all = conv_silu(0, q_ref, qh_ref, cwq_ref)
    k_all = conv_silu(1, k_ref, kh_ref, cwk_ref)
    v_all = conv_silu(2, v_ref, vh_ref, cwv_ref)

    gates = g_ref[...]
    g_all = -jnp.exp(alog_ref[...]) * _softplus(gates + dtb_ref[...])
    beta_all = jax.nn.sigmoid(gates)

    row = lax.broadcasted_iota(jnp.int32, (c, c), 0)
    col = lax.broadcasted_iota(jnp.int32, (c, c), 1)
    lower_ones = jnp.where(col <= row, 1.0, 0.0)
    minus_strict = jnp.where(col < row, -1.0, 0.0)
    eye = jnp.where(row == col, 1.0, 0.0)

    heads = range(GDN_HEADS)
    pairs = [(ch, h) for ch in range(chunks) for h in heads]
    rs = lambda ch: slice(ch * c, (ch + 1) * c)
    hs = lambda h: slice(h * HEAD_DIM, (h + 1) * HEAD_DIM)

    gcum_col = [_dot(lower_ones, g_all[rs(ch)]) for ch in range(chunks)]
    gcum_row = [_dot_nt(g_all[rs(ch)].T, lower_ones) for ch in range(chunks)]

    q, k, v, beta, g_c, g_last, e_g, decay = {}, {}, {}, {}, {}, {}, {}, {}
    for p in pairs:
        ch, h = p
        qq = q_all[rs(ch), hs(h)]
        kk = k_all[rs(ch), hs(h)]
        q[p] = qq * lax.rsqrt(jnp.sum(qq * qq, axis=-1, keepdims=True) + L2_EPS) * (HEAD_DIM ** -0.5)
        k[p] = kk * lax.rsqrt(jnp.sum(kk * kk, axis=-1, keepdims=True) + L2_EPS)
        v[p] = v_all[rs(ch), hs(h)]
        beta[p] = beta_all[rs(ch), GDN_HEADS + h:GDN_HEADS + h + 1]
        g_c[p] = gcum_col[ch][:, h:h + 1]
        g_r = gcum_row[ch][h:h + 1, :]
        g_last[p] = g_c[p][c - 1:c, :]
        e_g[p] = jnp.exp(g_c[p])
        decay[p] = jnp.exp(jnp.minimum(g_c[p] - g_r, 0.0)) * lower_ones

    k16 = {p: k[p].astype(BF16) for p in pairs}
    kk = {p: _dot_nt(k16[p], k16[p]) for p in pairs}
    qk = {p: _dot_nt(q[p].astype(BF16), k16[p]) for p in pairs}
    attn = {p: (qk[p] * decay[p]).astype(BF16) for p in pairs}
    pw = {p: (beta[p] * kk[p]) * (decay[p] * minus_strict) for p in pairs}
    t_mat = {p: eye + pw[p] for p in pairs}
    for _ in range(5):
        pw = {p: _dot(pw[p], pw[p]) for p in pairs}
        t_mat = {p: t_mat[p] + _dot(t_mat[p], pw[p]) for p in pairs}
    rhs = {p: jnp.concatenate([v[p] * beta[p], k[p] * (beta[p] * e_g[p])], axis=1).astype(BF16) for p in pairs}
    uw = {p: _dot(t_mat[p].astype(BF16), rhs[p]) for p in pairs}
    wq = {p: jnp.concatenate([uw[p][:, HEAD_DIM:], q[p] * e_g[p]], axis=0).astype(BF16) for p in pairs}
    k_dec = {p: (k[p] * jnp.exp(g_last[p] - g_c[p])).astype(BF16) for p in pairs}

    state = {h: state_ref[h] for h in heads}
    out = {}
    for ch in range(chunks):
        s16 = {h: state[h].astype(BF16) for h in heads}
        ws = {h: _dot(wq[ch, h], s16[h]) for h in heads}
        v_new = {h: (uw[ch, h][:, :HEAD_DIM] - ws[h][:c]).astype(BF16) for h in heads}
        out.update({(ch, h): ws[h][c:] + _dot(attn[ch, h], v_new[h]) for h in heads})
        state = {h: state[h] * jnp.exp(g_last[ch, h]) + _dot_tn(k_dec[ch, h], v_new[h]) for h in heads}
    for h in heads:
        state_ref[h] = state[h]

    for p in pairs:
        ch, h = p
        o = out[p]
        r = lax.rsqrt(jnp.mean(o * o, axis=-1, keepdims=True) + RMS_EPS)
        zz = z_ref[rs(ch), hs(h)]
        y = (o * r * nw_ref[...]) * (zz * jax.nn.sigmoid(zz))
        o_ref[rs(ch), hs(h)] = y.astype(o_ref.dtype)


def _gdn_mixer(proj, gates, conv_w, a_log, dt_bias, norm_w, layer, batch, seq, col0, *, chunks_per_step):
    rows = chunks_per_step * GDN_CHUNK
    assert seq % rows == 0
    nt = seq // rows
    cb0 = col0 // GDN_WIDTH
    hal = _halo_index(rows)
    tile = lambda cb: pl.BlockSpec((rows, GDN_WIDTH), lambda b, i: (b * nt + i, cb0 + cb))
    halo = lambda cb: pl.BlockSpec((HALO, GDN_WIDTH), lambda b, i: (hal(b * nt + i), cb0 + cb))
    cw = lambda cb: pl.BlockSpec((None, GDN_CONV, GDN_WIDTH), lambda b, i: (layer, 0, cb))
    rowvec = pl.BlockSpec((None, 1, HEAD_DIM), lambda b, i: (layer, 0, 0))
    return pl.pallas_call(
        functools.partial(_gdn_kernel, chunks=chunks_per_step),
        grid=(batch, nt),
        in_specs=[tile(0), tile(1), tile(2), tile(3), halo(0), halo(1), halo(2),
                  pl.BlockSpec((rows, HEAD_DIM), lambda b, i: (b * nt + i, 0)),
                  cw(0), cw(1), cw(2), rowvec, rowvec, rowvec],
        out_specs=pl.BlockSpec((rows, GDN_WIDTH), lambda b, i: (b * nt + i, 0)),
        out_shape=jax.ShapeDtypeStruct((batch * seq, GDN_WIDTH), BF16),
        scratch_shapes=[pltpu.VMEM((GDN_HEADS, HEAD_DIM, HEAD_DIM), F32),
                        pltpu.VMEM((3, HALO + rows, GDN_WIDTH), F32)],
        compiler_params=_cparams("parallel", "arbitrary"),
        name="gdn_mixer",
    )(proj, proj, proj, proj, proj, proj, proj, gates,
      conv_w, conv_w, conv_w, a_log, dt_bias, norm_w)


def _out_proj_kernel(ysb_ref, b_ref, c_ref, h_ref, ch_ref, hh_ref, cw_ref, ygdn_ref, w_ref, res_ref, o_ref,
                     xs_ref, *, tiles_per_seq):
    tm = b_ref.shape[0]
    first = pl.program_id(0) % tiles_per_seq == 0
    xs_ref[0:HALO, :] = jnp.where(first, 0.0, ch_ref[...] * hh_ref[...])
    xs_ref[HALO:HALO + tm, :] = c_ref[...] * h_ref[...]
    y_sc = (b_ref[...] * _causal_conv(xs_ref[...], cw_ref, SC_CONV)).astype(BF16)
    acc = res_ref[...]
    acc = acc + _dot(ysb_ref[...], w_ref[0:SB_WIDTH, :])
    acc = acc + _dot(y_sc, w_ref[SB_WIDTH:SB_WIDTH + SC_WIDTH, :])
    acc = acc + _dot(ygdn_ref[...], w_ref[SB_WIDTH + SC_WIDTH:, :])
    o_ref[...] = acc


def _out_proj(y_sb, proj, sc_conv_w, y_gdn, w, layer, res, seq, *, tm):
    m, d = res.shape
    kdim = w.shape[1]
    tm = min(tm, seq)
    assert seq % tm == 0 and m % seq == 0
    hal = _halo_index(tm)
    rows = lambda width, cb=0: pl.BlockSpec((tm, width), lambda i: (i, cb))
    halo = lambda cb: pl.BlockSpec((HALO, SC_WIDTH), lambda i: (hal(i), cb))
    return pl.pallas_call(
        functools.partial(_out_proj_kernel, tiles_per_seq=seq // tm),
        grid=(m // tm,),
        in_specs=[rows(SB_WIDTH), rows(SC_WIDTH, 0), rows(SC_WIDTH, 1), rows(SC_WIDTH, 2), halo(1), halo(2),
                  pl.BlockSpec((None, SC_CONV, SC_WIDTH), lambda i: (layer, 0, 0)),
                  rows(GDN_WIDTH),
                  pl.BlockSpec((None, kdim, d), lambda i: (layer, 0, 0)),
                  rows(d)],
        out_specs=rows(d),
        out_shape=jax.ShapeDtypeStruct((m, d), F32),
        scratch_shapes=[pltpu.VMEM((HALO + tm, SC_WIDTH), F32)],
        compiler_params=_cparams("parallel"),
        name="out_proj",
    )(y_sb, proj, proj, proj, proj, proj, sc_conv_w, y_gdn, w, res)


def _matmul_res_kernel(a_ref, w_ref, res_ref, nw_ref, o_ref, *, final_norm):
    k = pl.program_id(2)

    @pl.when(k == 0)
    def _():
        o_ref[...] = res_ref[...]

    o_ref[...] += _dot(a_ref[...], w_ref[...])

    if final_norm:
        @pl.when(k == pl.num_programs(2) - 1)
        def _():
            x = o_ref[...]
            r = lax.rsqrt(jnp.mean(x * x, axis=-1, keepdims=True) + RMS_EPS)
            o_ref[...] = x * r * nw_ref[...]


def _matmul_res(a, w, layer, res, norm_w, *, tm, tn, tk, final_norm):
    m, kdim = a.shape
    n = w.shape[2]
    tm, tn, tk = min(tm, m), min(tn, n), min(tk, kdim)
    assert m % tm == 0 and n % tn == 0 and kdim % tk == 0 and (tn == n or not final_norm)
    return pl.pallas_call(
        functools.partial(_matmul_res_kernel, final_norm=final_norm),
        grid=(m // tm, n // tn, kdim // tk),
        in_specs=[pl.BlockSpec((tm, tk), lambda i, j, k: (i, k)),
                  pl.BlockSpec((None, tk, tn), lambda i, j, k: (layer, k, j)),
                  pl.BlockSpec((tm, tn), lambda i, j, k: (i, j)),
                  pl.BlockSpec((1, tn), lambda i, j, k: (0, j))],
        out_specs=pl.BlockSpec((tm, tn), lambda i, j, k: (i, j)),
        out_shape=jax.ShapeDtypeStruct((m, n), F32),
        compiler_params=_cparams("parallel", "parallel", "arbitrary"),
        name="matmul_res",
    )(a, w, res, norm_w.reshape(1, n))


def kernel(x, norm1_w, w_in, sc_conv_w, gdn_conv_w, gdn_a_log, gdn_dt_bias, gdn_norm_w, w_out, norm2_w,
           w_up, w_down, final_norm_w):
    batch, seq, d = x.shape
    layers = norm1_w.shape[0]
    w_in16, w_out16, w_up16, w_down16 = (w.astype(BF16) for w in (w_in, w_out, w_up, w_down))
    pad_heads = lambda a: jnp.zeros((layers, 1, HEAD_DIM), F32).at[:, 0, :GDN_HEADS].set(a)
    a_log, dt_bias = pad_heads(gdn_a_log), pad_heads(gdn_dt_bias)
    gdn_nw = gdn_norm_w.reshape(layers, 1, HEAD_DIM)

    h = x.reshape(batch * seq, d)
    for l in range(layers):
        qkv_sb, proj, gates = _in_proj(h, norm1_w, w_in16, l, tm=1024)
        y_sb = _sb_attention(qkv_sb, batch, seq, blk=256, heads_per_step=6)
        y_gdn = _gdn_mixer(proj, gates, gdn_conv_w, a_log, dt_bias, gdn_nw, l, batch, seq, 3 * SC_WIDTH,
                           chunks_per_step=8)
        h = _out_proj(y_sb, proj, sc_conv_w, y_gdn, w_out16, l, h, seq, tm=512)
        hidden = _norm_matmul(h, norm2_w, w_up16, l, BF16, tm=1024, tn=1024, sq_relu=True)
        if l < layers - 1:
            h = _matmul_res(hidden, w_down16, l, h, final_norm_w, tm=1024, tn=1024, tk=4096, final_norm=False)
        else:
            h = _matmul_res(hidden, w_down16, l, h, final_norm_w, tm=512, tn=d, tk=2048, final_norm=True)
    return h.reshape(batch, seq, d)
```
